```python
import jax, jax.numpy as jnp
from jax import lax
import numpy as np

D_MODEL = 1024
BATCH = 2
SEQ = 8192
DEPTH = 1

CTX_LEN = 256
GRID_W = 64
N_DIRS = 2
RMS_EPS = 1e-6
SSM_D_INNER = 2 * D_MODEL
SSM_HEAD_DIM = 64
SSM_HEADS = SSM_D_INNER // SSM_HEAD_DIM
SSM_GROUPS = 4
SSM_HEADS_PER_GROUP = SSM_HEADS // SSM_GROUPS
SSM_STATE = 128
SSM_CONV_K = 3
SSM_CONV_DIM = SSM_D_INNER + 2 * SSM_GROUPS * SSM_STATE
SSM_CHUNK = 128
HG_HEADS = 8
HG_KEY_DIM = 128
HG_VAL_DIM = D_MODEL // HG_HEADS
HG_KEY = HG_HEADS * HG_KEY_DIM
HG_VAL = HG_HEADS * HG_VAL_DIM
HG_CHUNK = 64
MOE_GROUPS = 4
MOE_EXPERTS_PER_GROUP = 8
MOE_EXPERTS = MOE_GROUPS * MOE_EXPERTS_PER_GROUP
MOE_TOP_K = 2
MOE_D_FF = 512
MOE_BLOCK = 128
COL_SIZES = (SSM_D_INNER, SSM_CONV_DIM, N_DIRS * SSM_HEADS, HG_KEY, N_DIRS * HG_KEY, HG_VAL, HG_VAL, D_MODEL, D_MODEL)
N_IN = sum(COL_SIZES)

kernel_name = 'hybrid_ssd_hgrn2_hmoe_block'


def rms_norm(x, g):
    xf = x.astype(jnp.float32)
    y = xf * lax.rsqrt(jnp.mean(xf * xf, axis=-1, keepdims=True) + RMS_EPS)
    return (y * g.astype(jnp.float32)).astype(x.dtype)


def dwconv_centred(u, w, b):
    out = lax.conv_general_dilated(u, w[:, None, :].astype(u.dtype), window_strides=(1,), padding='SAME',
                                   dimension_numbers=('NWC', 'WIO', 'NWC'), feature_group_count=u.shape[-1])
    return out + b.astype(u.dtype)


def make_dirs(ctx_f, lat_f, ctx_b, lat_b):
    fwd = jnp.concatenate([ctx_f, lat_f], axis=1)
    bwd = jnp.concatenate([jnp.flip(ctx_b, 1), jnp.flip(lat_b, 1)], axis=1)
    return jnp.concatenate([fwd, bwd], axis=0)


def merge_dirs(y, n_ctx):
    nb = y.shape[0] // 2
    fwd, bwd = y[:nb], y[nb:]
    bwd = jnp.concatenate([jnp.flip(bwd[:, :n_ctx], 1), jnp.flip(bwd[:, n_ctx:], 1)], axis=1)
    return fwd + bwd


def raster_to_cols(t, rows):
    b, s = t.shape[:2]
    return t.reshape(b, rows, GRID_W, *t.shape[2:]).swapaxes(1, 2).reshape(t.shape)


def cols_to_raster(t, rows):
    b, s = t.shape[:2]
    return t.reshape(b, GRID_W, rows, *t.shape[2:]).swapaxes(1, 2).reshape(t.shape)


def segsum_exp(a):
    t = a.shape[-1]
    cs = jnp.cumsum(a, axis=-1)
    mask = jnp.tril(jnp.ones((t, t), dtype=bool))
    return jnp.exp(jnp.where(mask, cs[..., :, None] - cs[..., None, :], -jnp.inf))


def ssd_scan(x, dt, a, bm, cm):
    f32 = jnp.float32
    z, l, g, j, p = x.shape
    n = bm.shape[-1]
    q = SSM_CHUNK
    nc = l // q
    dt = dt.astype(f32)
    xdt = (x.astype(f32) * dt[..., None]).reshape(z, nc, q, g, j, p)
    adt = (dt * a[:, None]).reshape(z, nc, q, g, j).transpose(0, 3, 4, 1, 2)
    bc = bm.astype(f32).reshape(z, nc, q, g, n)
    cc = cm.astype(f32).reshape(z, nc, q, g, n)
    a_cs = jnp.cumsum(adt, axis=-1)
    cb = jnp.einsum('zclgn,zcsgn->zgcls', cc, bc)
    w_diag = cb[:, :, None] * segsum_exp(adt)
    y_diag = jnp.einsum('zgjcls,zcsgjp->zclgjp', w_diag, xdt)
    decay_states = jnp.exp(a_cs[..., -1:] - a_cs)
    states = jnp.einsum('zclgn,zgjcl,zclgjp->zcgjpn', bc, decay_states, xdt)
    states = jnp.concatenate([jnp.zeros_like(states[:, :1]), states], axis=1)
    chunk_decay = segsum_exp(jnp.pad(a_cs[..., -1], ((0, 0), (0, 0), (0, 0), (1, 0))))
    states_in = jnp.einsum('zgjab,zbgjpn->zagjpn', chunk_decay, states)[:, :-1]
    y_off = jnp.einsum('zclgn,zcgjpn,zgjcl->zclgjp', cc, states_in, jnp.exp(a_cs))
    return (y_diag + y_off).reshape(z, l, g, j, p)


def gla_scan(q, k, v, log_f):
    f32 = jnp.float32
    z, l, h, dk = q.shape
    dv = v.shape[-1]
    c = HG_CHUNK
    nc = l // c

    def chunks(t):
        return t.astype(f32).reshape(z, nc, c, h, t.shape[-1]).transpose(1, 0, 3, 2, 4)

    mask = jnp.tril(jnp.ones((c, c), dtype=bool))[:, :, None]

    def step(state, inp):
        qc, kc, vc, lc = inp
        b = jnp.cumsum(lc, axis=2)
        inter = jnp.einsum('zhtk,zhkv->zhtv', qc * jnp.exp(b), state)
        decay = jnp.exp(jnp.where(mask, b[:, :, :, None, :] - b[:, :, None, :, :], -jnp.inf))
        att = jnp.einsum('zhtk,zhsk,zhtsk->zhts', qc, kc, decay)
        out = inter + jnp.einsum('zhts,zhsv->zhtv', att, vc)
        b_end = b[:, :, -1]
        state = jnp.exp(b_end)[..., None] * state + jnp.einsum(
            'zhsk,zhsv->zhkv', kc * jnp.exp(b_end[:, :, None] - b), vc)
        return state, out

    s0 = jnp.zeros((z, h, dk, dv), f32)
    _, out = lax.scan(step, s0, (chunks(q), chunks(k), chunks(v), chunks(log_f)))
    return out.transpose(1, 0, 3, 2, 4).reshape(z, l, h, dv)


def token_mixer(hl, hc, w_in, conv_w, conv_b, dt_bias, a_log, d_skip, g_ssm_norm, lb, g_hg_norm,
                w_branch_ssm, w_branch_hg, w_out, rows, with_ctx):
    f32 = jnp.float32
    nb, n_lat, _ = hl.shape
    n_ctx = hc.shape[1]
    splits = np.cumsum(COL_SIZES)[:-1].tolist()
    z_l, xbc_l, dtr_l, qr_l, fr_l, ir_l, og_l, gm_l, gh_l = jnp.split(hl @ w_in, splits, axis=-1)
    z_c, xbc_c, dtr_c, qr_c, fr_c, ir_c, og_c, gm_c, gh_c = jnp.split(hc @ w_in, splits, axis=-1)

    xbc_l = jax.nn.silu(dwconv_centred(xbc_l, conv_w, conv_b))
    xbc_c = jax.nn.silu(dwconv_centred(xbc_c, conv_w, conv_b))

    def ssm_split(u):
        b_, t_ = u.shape[:2]
        xs, bm, cm = jnp.split(u, [SSM_D_INNER, SSM_D_INNER + SSM_GROUPS * SSM_STATE], axis=-1)
        return (xs.reshape(b_, t_, SSM_GROUPS, SSM_HEADS_PER_GROUP, SSM_HEAD_DIM),
                bm.reshape(b_, t_, SSM_GROUPS, SSM_STATE), cm.reshape(b_, t_, SSM_GROUPS, SSM_STATE))

    def ssm_dt(u):
        b_, t_ = u.shape[:2]
        d = jax.nn.softplus(u.reshape(b_, t_, N_DIRS, SSM_HEADS).astype(f32) + dt_bias.astype(f32))
        return d.reshape(b_, t_, N_DIRS, SSM_GROUPS, SSM_HEADS_PER_GROUP)

    xs_l, bm_l, cm_l = ssm_split(xbc_l)
    xs_c, bm_c, cm_c = ssm_split(xbc_c)
    dt_l, dt_c = ssm_dt(dtr_l), ssm_dt(dtr_c)
    xs = make_dirs(xs_c, xs_l, xs_c, xs_l)
    bm = make_dirs(bm_c, bm_l, bm_c, bm_l)
    cm = make_dirs(cm_c, cm_l, cm_c, cm_l)
    dt = make_dirs(dt_c[:, :, 0], dt_l[:, :, 0], dt_c[:, :, 1], dt_l[:, :, 1])
    a = jnp.repeat(-jnp.exp(a_log.astype(f32)), nb, axis=0).reshape(N_DIRS * nb, SSM_GROUPS, SSM_HEADS_PER_GROUP)
    dsk = jnp.repeat(d_skip.astype(f32), nb, axis=0).reshape(N_DIRS * nb, 1, SSM_GROUPS, SSM_HEADS_PER_GROUP, 1)
    y = ssd_scan(xs, dt, a, bm, cm) + xs.astype(f32) * dsk
    y = merge_dirs(y, n_ctx).reshape(nb, n_ctx + n_lat, SSM_D_INNER).astype(hl.dtype)
    ys_c, ys_l = y[:, :n_ctx], y[:, n_ctx:]

    def hg_split(q, f, i):
        b_, t_ = q.shape[:2]
        qh = jax.nn.silu(q).reshape(b_, t_, HG_HEADS, HG_KEY_DIM)
        fz = f.reshape(b_, t_, N_DIRS, HG_KEY).astype(f32)
        log_f = jnp.log(lb + (1.0 - lb) * jax.nn.sigmoid(fz))
        k = (1.0 - lb) * jax.nn.sigmoid(-fz)
        shp = (b_, t_, N_DIRS, HG_HEADS, HG_KEY_DIM)
        return qh, k.reshape(shp), log_f.reshape(shp), i.reshape(b_, t_, HG_HEADS, HG_VAL_DIM)

    q_l, k_l, lf_l, v_l = hg_split(raster_to_cols(qr_l, rows), raster_to_cols(fr_l, rows), raster_to_cols(ir_l, rows))
    q_c, k_c, lf_c, v_c = hg_split(qr_c, fr_c, ir_c)
    o = gla_scan(make_dirs(q_c, q_l, q_c, q_l),
                 make_dirs(k_c[:, :, 0], k_l[:, :, 0], k_c[:, :, 1], k_l[:, :, 1]),
                 make_dirs(v_c, v_l, v_c, v_l),
                 make_dirs(lf_c[:, :, 0], lf_l[:, :, 0], lf_c[:, :, 1], lf_l[:, :, 1]))
    o = merge_dirs(o, n_ctx).astype(hl.dtype)
    o_c = o[:, :n_ctx]
    o_l = cols_to_raster(o[:, n_ctx:], rows)

    def merge(ys, z, oh, og, gm, gh):
        b_, t_ = ys.shape[:2]
        yz = (ys * jax.nn.silu(z)).reshape(b_, t_, SSM_GROUPS, SSM_D_INNER // SSM_GROUPS)
        ys = rms_norm(yz, g_ssm_norm.reshape(SSM_GROUPS, -1)).reshape(b_, t_, SSM_D_INNER)
        oh = rms_norm(oh, g_hg_norm) * jax.nn.silu(og.reshape(b_, t_, HG_HEADS, HG_VAL_DIM))
        mixed = (jax.nn.sigmoid(gm) * (ys @ w_branch_ssm)
                 + jax.nn.sigmoid(gh) * (oh.reshape(b_, t_, HG_VAL) @ w_branch_hg))
        return mixed @ w_out

    out_l = merge(ys_l, z_l, o_l, og_l, gm_l, gh_l)
    out_c = merge(ys_c, z_c, o_c, og_c, gm_c, gh_c) if with_ctx else None
    return out_l, out_c


def hier_moe(h, w_gr, b_gr, w_er, b_er, w_gate, w_up, w_down):
    f32 = jnp.float32
    n, d = h.shape
    g_prob = jax.nn.softmax((h @ w_gr).astype(f32) + b_gr.astype(f32), axis=-1)
    p_grp, grp = lax.top_k(g_prob, 1)
    e_logits = jnp.einsum('nd,gde->nge', h, w_er).astype(f32) + b_er.astype(f32)
    e_logits = e_logits[jnp.arange(n), grp[:, 0]]
    top_v, top_i = lax.top_k(e_logits, MOE_TOP_K)
    weight = p_grp * jax.nn.softmax(top_v, axis=-1)
    expert = grp * MOE_EXPERTS_PER_GROUP + top_i
    flat_e = expert.reshape(-1)
    flat_w = weight.reshape(-1).astype(h.dtype)
    flat_tok = jnp.repeat(jnp.arange(n, dtype=jnp.int32), MOE_TOP_K)
    m = n * MOE_TOP_K
    r = MOE_BLOCK
    counts = jnp.bincount(flat_e, length=MOE_EXPERTS)
    padded = (counts + r - 1) // r * r
    pad_end = jnp.cumsum(padded)
    pad_start = pad_end - padded
    cnt_start = jnp.cumsum(counts) - counts
    order = jnp.argsort(flat_e)
    e_sorted = flat_e[order]
    dest = pad_start[e_sorted] + jnp.arange(m) - cnt_start[e_sorted]
    cap = -(-(m + MOE_EXPERTS * r) // r) * r
    n_blk = cap // r
    tok_buf = jnp.zeros((cap,), jnp.int32).at[dest].set(flat_tok[order])
    w_buf = jnp.zeros((cap,), h.dtype).at[dest].set(flat_w[order])
    blk_e = jnp.minimum(jnp.searchsorted(pad_end, jnp.arange(n_blk) * r, side='right'), MOE_EXPERTS - 1)
    xb = h[tok_buf].reshape(n_blk, r, d)

    def expert_block(args):
        xr, e = args
        return (jax.nn.silu(xr @ w_gate[e]) * (xr @ w_up[e])) @ w_down[e]

    yb = lax.map(expert_block, (xb, blk_e)).reshape(cap, d)
    return jnp.zeros_like(h).at[tok_buf].add(yb * w_buf[:, None])


def setup_inputs(seed: int = 0) -> dict:
    key = jax.random.key(seed)
    ks = iter(jax.random.split(key, 32))
    f32 = jnp.float32
    D = D_MODEL

    def nrm(shape, scale):
        return jax.random.normal(next(ks), shape, f32) * scale

    u_dt = jax.random.uniform(next(ks), (DEPTH, N_DIRS, SSM_HEADS), f32)
    dt0 = jnp.exp(u_dt * (np.log(0.1) - np.log(0.001)) + np.log(0.001)).astype(f32)
    return {
        'x': nrm((BATCH, SEQ, D), 1.0),
        'c': nrm((BATCH, D), 1.0),
        'ctx': nrm((BATCH, CTX_LEN, D), 1.0),
        'c_ctx': nrm((D,), 1.0),
        'w_ada': nrm((DEPTH, D, 6 * D), 0.5 * D ** -0.5),
        'b_ada': nrm((DEPTH, 6 * D), 0.01),
        'g_pre_mix': 1.0 + nrm((DEPTH, D), 0.02),
        'g_post_mix': 1.0 + nrm((DEPTH, D), 0.02),
        'g_pre_ffn': 1.0 + nrm((DEPTH, D), 0.02),
        'g_post_ffn': 1.0 + nrm((DEPTH, D), 0.02),
        'w_in': nrm((DEPTH, D, N_IN), D ** -0.5),
        'conv_w': nrm((DEPTH, SSM_CONV_K, SSM_CONV_DIM), SSM_CONV_K ** -0.5),
        'conv_b': nrm((DEPTH, SSM_CONV_DIM), 0.01),
        'dt_bias': dt0 + jnp.log(-jnp.expm1(-dt0)),
        'a_log': jnp.log(jax.random.uniform(next(ks), (DEPTH, N_DIRS, SSM_HEADS), f32, 1.0, 16.0)),
        'd_skip': 1.0 + nrm((DEPTH, N_DIRS, SSM_HEADS), 0.1),
        'g_ssm_norm': 1.0 + nrm((DEPTH, SSM_D_INNER), 0.02),
        'hg_lb': nrm((DEPTH + 1, N_DIRS, HG_KEY), 0.1),
        'g_hg_norm': 1.0 + nrm((DEPTH, HG_VAL_DIM), 0.02),
        'w_branch_ssm': nrm((DEPTH, SSM_D_INNER, D), SSM_D_INNER ** -0.5),
        'w_branch_hg': nrm((DEPTH, HG_VAL, D), HG_VAL ** -0.5),
        'w_out': nrm((DEPTH, D, D), D ** -0.5),
        'w_group_router': nrm((DEPTH, D, MOE_GROUPS), D ** -0.5),
        'b_group_router': nrm((DEPTH, MOE_GROUPS), 0.01),
        'w_expert_router': nrm((DEPTH, MOE_GROUPS, D, MOE_EXPERTS_PER_GROUP), D ** -0.5),
        'b_expert_router': nrm((DEPTH, MOE_GROUPS, MOE_EXPERTS_PER_GROUP), 0.01),
        'w_gate': nrm((DEPTH, MOE_EXPERTS, D, MOE_D_FF), D ** -0.5),
        'w_up': nrm((DEPTH, MOE_EXPERTS, D, MOE_D_FF), D ** -0.5),
        'w_down': nrm((DEPTH, MOE_EXPERTS, MOE_D_FF, D), MOE_D_FF ** -0.5),
    }


def reference(x, c, ctx, c_ctx, w_ada, b_ada, g_pre_mix, g_post_mix, g_pre_ffn, g_post_ffn, w_in,
              conv_w, conv_b, dt_bias, a_log, d_skip, g_ssm_norm, hg_lb, g_hg_norm, w_branch_ssm,
              w_branch_hg, w_out, w_group_router, b_group_router, w_expert_router, b_expert_router,
              w_gate, w_up, w_down):
    nb, n_lat, d = x.shape
    rows = n_lat // GRID_W
    lb_all = jnp.cumsum(jax.nn.softmax(hg_lb.astype(jnp.float32), axis=0), axis=0)
    xl, xc = x, ctx
    for layer in range(DEPTH):
        last = layer == DEPTH - 1
        mod_l = (jax.nn.silu(c) @ w_ada[layer] + b_ada[layer])[:, None, :]
        mod_c = (jax.nn.silu(c_ctx) @ w_ada[layer] + b_ada[layer])[None, None, :]
        sh1_l, sc1_l, gt1_l, sh2_l, sc2_l, gt2_l = jnp.split(mod_l, 6, axis=-1)
        sh1_c, sc1_c, gt1_c, sh2_c, sc2_c, gt2_c = jnp.split(mod_c, 6, axis=-1)

        hl = rms_norm(xl, g_pre_mix[layer]) * (1.0 + sc1_l) + sh1_l
        hc = rms_norm(xc, g_pre_mix[layer]) * (1.0 + sc1_c) + sh1_c
        mix_l, mix_c = token_mixer(hl, hc, w_in[layer], conv_w[layer], conv_b[layer], dt_bias[layer],
                                   a_log[layer], d_skip[layer], g_ssm_norm[layer], lb_all[layer],
                                   g_hg_norm[layer], w_branch_ssm[layer], w_branch_hg[layer], w_out[layer],
                                   rows, not last)
        xl = xl + gt1_l * rms_norm(mix_l, g_post_mix[layer])

        hl = rms_norm(xl, g_pre_ffn[layer]) * (1.0 + sc2_l) + sh2_l
        moe_w = (w_group_router[layer], b_group_router[layer], w_expert_router[layer], b_expert_router[layer],
                 w_gate[layer], w_up[layer], w_down[layer])
        if last:
            ffn_l = hier_moe(hl.reshape(-1, d), *moe_w).reshape(xl.shape)
        else:
            xc = xc + gt1_c * rms_norm(mix_c, g_post_mix[layer])
            hc = rms_norm(xc, g_pre_ffn[layer]) * (1.0 + sc2_c) + sh2_c
            ffn = hier_moe(jnp.concatenate([hl.reshape(-1, d), hc.reshape(-1, d)], axis=0), *moe_w)
            ffn_l = ffn[:nb * n_lat].reshape(xl.shape)
            xc = xc + gt2_c * rms_norm(ffn[nb * n_lat:].reshape(xc.shape), g_post_ffn[layer])
        xl = xl + gt2_l * rms_norm(ffn_l, g_post_ffn[layer])
    return xl
```

```python
import functools

import jax
import jax.numpy as jnp
from jax import lax
from jax.experimental import pallas as pl
from jax.experimental.pallas import tpu as pltpu

F32 = jnp.float32
BF16 = jnp.bfloat16
I32 = jnp.int32
HIGHEST = lax.Precision.HIGHEST

D_MODEL = 1024
GRID_W = 64
RMS_EPS = 1e-6
SSM_D_INNER = 2048
SSM_HEAD_DIM = 64
SSM_HEADS = 32
SSM_GROUPS = 4
SSM_HEADS_PER_GROUP = 8
SSM_STATE = 128
SSM_GROUP_WIDTH = SSM_HEADS_PER_GROUP * SSM_HEAD_DIM
SSM_CONV_DIM = SSM_D_INNER + 2 * SSM_GROUPS * SSM_STATE
HG_HEADS = 8
HG_DIM = 128
HG_KEY = HG_HEADS * HG_DIM
MOE_GROUPS = 4
MOE_EXPERTS_PER_GROUP = 8
MOE_EXPERTS = 32
MOE_D_FF = 512

LANES = 128
SUBLANES = 8
VMEM_LIMIT_BYTES = 56 * 1024 * 1024

TOK_TILE = 256
SSD_CHUNK = 128
HG_CHUNK = 64
HG_SUB = 16
MOE_ROWS = 256
RANK_TILE = 512


def _cparams(n_axes):
    return pltpu.CompilerParams(dimension_semantics=("arbitrary",) * n_axes,
                                vmem_limit_bytes=VMEM_LIMIT_BYTES)


def _sigmoid(x):
    return 1.0 / (1.0 + jnp.exp(-x))


def _silu(x):
    return x * _sigmoid(x)


def _rms(v, g):
    ms = jnp.mean(v * v, axis=-1, keepdims=True)
    return v * lax.rsqrt(ms + RMS_EPS) * g


def _dot(a, b):
    return jnp.dot(a, b, preferred_element_type=F32)


def _dot_nt(a, b):
    return lax.dot_general(a, b, (((1,), (1,)), ((), ())), preferred_element_type=F32)


def _dot_tn(a, b):
    return lax.dot_general(a, b, (((0,), (0,)), ((), ())), preferred_element_type=F32)


def _mod_kernel(c_ref, w_ref, b_ref, o_ref):
    s = _silu(c_ref[...])
    o_ref[...] = jnp.dot(s, w_ref[...], precision=HIGHEST, preferred_element_type=F32) + b_ref[...]


def _modulation(cc, w_ada, b_ada):
    n = w_ada.shape[1]
    tn = 1024
    return pl.pallas_call(
        _mod_kernel,
        grid=(n // tn,),
        in_specs=[pl.BlockSpec((SUBLANES, D_MODEL), lambda j: (0, 0)),
                  pl.BlockSpec((D_MODEL, tn), lambda j: (0, j)),
                  pl.BlockSpec((1, tn), lambda j: (0, j))],
        out_specs=pl.BlockSpec((SUBLANES, tn), lambda j: (0, j)),
        out_shape=jax.ShapeDtypeStruct((SUBLANES, n), F32),
        compiler_params=_cparams(1),
        name="adaln_mod",
    )(cc, w_ada, b_ada.reshape(1, n))


def _norm_kernel(x_ref, ctx_ref, mod_ref, g_ref, all_ref, lat_ref, *, ctx_row):
    b = pl.program_id(0)
    i = pl.program_id(1)
    g = g_ref[...]

    def nm(v, row):
        sh = mod_ref[pl.ds(row, 1), pl.ds(0, D_MODEL)]
        sc = mod_ref[pl.ds(row, 1), pl.ds(D_MODEL, D_MODEL)]
        return (_rms(v, g) * (1.0 + sc) + sh).astype(BF16)

    @pl.when(i == 0)
    def _():
        all_ref[...] = nm(ctx_ref[...], ctx_row)

    @pl.when(i > 0)
    def _():
        h = nm(x_ref[...], b)
        all_ref[...] = h
        lat_ref[...] = h


def _pre_norm(x, ctx, mod, g):
    nb, n_lat, d = x.shape
    n_ctx = ctx.shape[1]
    assert n_ctx == TOK_TILE and n_lat % TOK_TILE == 0
    nt = n_lat // TOK_TILE
    return pl.pallas_call(
        functools.partial(_norm_kernel, ctx_row=nb),
        grid=(nb, nt + 1),
        in_specs=[pl.BlockSpec((None, TOK_TILE, d), lambda b, i: (b, jnp.maximum(i - 1, 0), 0)),
                  pl.BlockSpec((None, TOK_TILE, d), lambda b, i: (b, 0, 0)),
                  pl.BlockSpec(mod.shape, lambda b, i: (0, 0)),
                  pl.BlockSpec((1, d), lambda b, i: (0, 0))],
        out_specs=[pl.BlockSpec((None, TOK_TILE, d), lambda b, i: (b, i, 0)),
                   pl.BlockSpec((None, TOK_TILE, d), lambda b, i: (b, jnp.maximum(i - 1, 0), 0))],
        out_shape=[jax.ShapeDtypeStruct((nb, n_ctx + n_lat, d), BF16),
                   jax.ShapeDtypeStruct((nb, n_lat, d), BF16)],
        compiler_params=_cparams(2),
        name="pre_norm",
    )(x, ctx, mod, g)


XBC_COL_TILE = 512


def _xbc_kernel(h_ref, hp_ref, hn_ref, w_ref, wdt_ref, cw_ref, cb_ref, dtb_ref, xbc_ref, dt_ref,
                *, n_tiles):
    i = pl.program_id(1)
    h = h_ref[...]
    hp = hp_ref[...]
    hn = hn_ref[...]
    tm = h.shape[0]
    prev_ok = (i >= 2).astype(F32)
    next_ok = jnp.logical_and(i >= 1, i < n_tiles - 1).astype(F32)
    row = lax.broadcasted_iota(I32, (tm, XBC_COL_TILE), 0)
    for c in range(SSM_CONV_DIM // XBC_COL_TILE):
        sl = slice(c * XBC_COL_TILE, (c + 1) * XBC_COL_TILE)
        w = w_ref[:, sl]
        raw = _dot(h, w)
        rp = _dot(hp, w)[SUBLANES - 1:SUBLANES] * prev_ok
        rn = _dot(hn, w)[0:1] * next_ok
        up = jnp.where(row == 0, rp, pltpu.roll(raw, 1, 0))
        dn = jnp.where(row == tm - 1, rn, pltpu.roll(raw, tm - 1, 0))
        conv = cw_ref[0:1, sl] * up + cw_ref[1:2, sl] * raw + cw_ref[2:3, sl] * dn + cb_ref[:, sl]
        xbc_ref[:, sl] = _silu(conv).astype(BF16)
    u = _dot(h, wdt_ref[...]) + dtb_ref[...]
    dt_ref[...] = jnp.maximum(u, 0.0) + jnp.log(1.0 + jnp.exp(-jnp.abs(u)))


def _ssd_inputs(h_all, w_xbc, w_dt, conv_w, conv_b, dt_bias):
    nb, ltot, d = h_all.shape
    nt = ltot // TOK_TILE
    n8 = ltot // SUBLANES
    per = TOK_TILE // SUBLANES
    return pl.pallas_call(
        functools.partial(_xbc_kernel, n_tiles=nt),
        grid=(nb, nt),
        in_specs=[pl.BlockSpec((None, TOK_TILE, d), lambda b, i: (b, i, 0)),
                  pl.BlockSpec((None, SUBLANES, d), lambda b, i: (b, jnp.maximum(i * per - 1, 0), 0)),
                  pl.BlockSpec((None, SUBLANES, d), lambda b, i: (b, jnp.minimum((i + 1) * per, n8 - 1), 0)),
                  pl.BlockSpec(w_xbc.shape, lambda b, i: (0, 0)),
                  pl.BlockSpec(w_dt.shape, lambda b, i: (0, 0)),
                  pl.BlockSpec(conv_w.shape, lambda b, i: (0, 0)),
                  pl.BlockSpec(conv_b.shape, lambda b, i: (0, 0)),
                  pl.BlockSpec(dt_bias.shape, lambda b, i: (0, 0))],
        out_specs=[pl.BlockSpec((None, TOK_TILE, SSM_CONV_DIM), lambda b, i: (b, i, 0)),
                   pl.BlockSpec((None, TOK_TILE, 2 * LANES), lambda b, i: (b, i, 0))],
        out_shape=[jax.ShapeDtypeStruct((nb, ltot, SSM_CONV_DIM), BF16),
                   jax.ShapeDtypeStruct((nb, ltot, 2 * LANES), F32)],
        compiler_params=_cparams(2),
        name="ssd_inputs",
    )(h_all, h_all, h_all, w_xbc, w_dt, conv_w, conv_b, dt_bias)


def _ssd_kernel(xbc_ref, dt_ref, alog_ref, dskip_ref, e_ref, y_ref, state):
    d = pl.program_id(1)
    step = pl.program_id(2)
    q = SSD_CHUNK

    @pl.when(step == 0)
    def _():
        state[...] = jnp.zeros(state.shape, F32)

    lane = lax.broadcasted_iota(I32, (1, LANES), 1)
    a = jnp.where(lane < SSM_HEADS, -jnp.exp(alog_ref[...]), 0.0)
    dt = dt_ref[...]
    adt = dt * a
    r = lax.broadcasted_iota(I32, (q, q), 0)
    c = lax.broadcasted_iota(I32, (q, q), 1)
    mask = jnp.where(d == 0, r - c, c - r) >= 0
    cs = jnp.dot(mask.astype(F32), adt, precision=HIGHEST, preferred_element_type=F32)
    tot = jnp.sum(adt, axis=0, keepdims=True)
    cs_t = cs.T
    lhs = jnp.concatenate([dt, jnp.exp(cs), jnp.exp(tot - cs),
                           jnp.broadcast_to(jnp.exp(tot), (SUBLANES, LANES))], axis=0).astype(BF16)
    lane_h = lax.broadcasted_iota(I32, (q, LANES), 1)
    for g in range(SSM_GROUPS):
        gs = slice(g * SSM_GROUP_WIDTH, (g + 1) * SSM_GROUP_WIDTH)
        ex = _dot(lhs, e_ref[:, gs])
        dt_e, e1_e, e2_e, etot_e = ex[0:q], ex[q:2 * q], ex[2 * q:3 * q], ex[3 * q:3 * q + 1]
        x = xbc_ref[:, gs].astype(F32)
        bg = xbc_ref[:, SSM_D_INNER + g * SSM_STATE:SSM_D_INNER + (g + 1) * SSM_STATE]
        cg = xbc_ref[:, SSM_D_INNER + (SSM_GROUPS + g) * SSM_STATE:
                     SSM_D_INNER + (SSM_GROUPS + g + 1) * SSM_STATE]
        xdt = x * dt_e
        cb = _dot_nt(cg, bg)
        s_in = state[g]
        y_off = _dot(cg, s_in.astype(BF16)) * e1_e
        state[g] = s_in * etot_e + _dot_tn(bg, (xdt * e2_e).astype(BF16))
        parts = []
        for m in range(SSM_HEADS_PER_GROUP // 2):
            xp = xdt[:, m * LANES:(m + 1) * LANES]
            acc = None
            for half in range(2):
                hd = g * SSM_HEADS_PER_GROUP + 2 * m + half
                col = cs[:, hd:hd + 1]
                rowv = cs_t[hd:hd + 1, :]
                decay = jnp.exp(jnp.minimum(col - rowv, 0.0))
                w = jnp.where(mask, cb * decay, 0.0).astype(BF16)
                sel = (lane_h < SSM_HEAD_DIM) if half == 0 else (lane_h >= SSM_HEAD_DIM)
                t = _dot(w, jnp.where(sel, xp, 0.0).astype(BF16))
                acc = t if acc is None else acc + t
            parts.append(acc)
        y = jnp.concatenate(parts, axis=1) + y_off + x * dskip_ref[:, gs]
        y_ref[:, gs] = y.astype(BF16)


def _ssd_scan(xbc, dt, a_log_p, dskip_e, expand):
    nb, ltot, _ = xbc.shape
    q = SSD_CHUNK
    nc = ltot // q
    n_ctx_chunks = TOK_TILE // q

    def chunk(d, s):
        rev = jnp.where(s < n_ctx_chunks, n_ctx_chunks - 1 - s, nc + n_ctx_chunks - 1 - s)
        return jnp.where(d == 0, s, rev)

    return pl.pallas_call(
        _ssd_kernel,
        grid=(nb, 2, nc),
        in_specs=[pl.BlockSpec((None, q, SSM_CONV_DIM), lambda b, d, s: (b, chunk(d, s), 0)),
                  pl.BlockSpec((None, q, LANES), lambda b, d, s: (b, chunk(d, s), d)),
                  pl.BlockSpec((None, 1, LANES), lambda b, d, s: (d, 0, 0)),
                  pl.BlockSpec((None, 1, SSM_D_INNER), lambda b, d, s: (d, 0, 0)),
                  pl.BlockSpec(expand.shape, lambda b, d, s: (0, 0))],
        out_specs=pl.BlockSpec((None, None, q, SSM_D_INNER), lambda b, d, s: (d, b, chunk(d, s), 0)),
        out_shape=jax.ShapeDtypeStruct((2, nb, ltot, SSM_D_INNER), BF16),
        scratch_shapes=[pltpu.VMEM((SSM_GROUPS, SSM_STATE, SSM_GROUP_WIDTH), F32)],
        compiler_params=_cparams(3),
        name="ssd_scan",
    )(xbc, dt, a_log_p, dskip_e, expand)


def _hg_chunk(q, k, lf, v, s_t, tril, lane_a, row_a):
    c = HG_CHUNK
    b = jnp.dot(tril, lf, precision=HIGHEST, preferred_element_type=F32)
    btot = b[c - 1:c]
    inter = _dot_nt((q * jnp.exp(b)).astype(BF16), s_t.astype(BF16))
    lane8 = lax.broadcasted_iota(I32, (SUBLANES, c), 1)
    lane_sub = lax.broadcasted_iota(I32, (HG_SUB, c), 1)
    rows = []
    for i in range(c // HG_SUB):
        lo = i * HG_SUB
        bi = b[lo:lo + HG_SUB]
        qi = q[lo:lo + HG_SUB]
        top = jnp.zeros((SUBLANES, c), F32)
        bot = jnp.zeros((SUBLANES, c), F32)
        for j in range(HG_SUB):
            s = lo + j
            if j < SUBLANES:
                p = qi * k[s:s + 1] * jnp.exp(jnp.minimum(bi - b[s:s + 1], 0.0))
                top = jnp.where(lane8 == s, jnp.sum(p[0:SUBLANES], axis=1, keepdims=True), top)
                bot = jnp.where(lane8 == s, jnp.sum(p[SUBLANES:], axis=1, keepdims=True), bot)
            else:
                p = qi[SUBLANES:] * k[s:s + 1] * jnp.exp(jnp.minimum(bi[SUBLANES:] - b[s:s + 1], 0.0))
                col = jnp.sum(p, axis=1, keepdims=True)
                bot = jnp.where(lane8 == s, col, bot)
        a_row = jnp.concatenate([top, bot], axis=0)
        if i > 0:
            bref = b[lo - 1:lo]
            qt = qi * jnp.exp(bi - bref)
            kt = k[0:lo] * jnp.exp(bref - b[0:lo])
            ktp = jnp.concatenate([kt, jnp.zeros((c - lo, HG_DIM), F32)], axis=0)
            a_off = _dot_nt(qt.astype(BF16), ktp.astype(BF16))
            a_row = jnp.where(lane_sub < lo, a_off, a_row)
        rows.append(a_row)
    a = jnp.concatenate(rows, axis=0)
    a = jnp.where(row_a >= lane_a, a, 0.0)
    o = inter + _dot(a.astype(BF16), v.astype(BF16))
    kt2 = k * jnp.exp(btot - b)
    s_new = s_t * jnp.exp(btot) + _dot_tn(v.astype(BF16), kt2.astype(BF16))
    return o, s_new


def _hg_kernel(h_ref, wq_ref, wf_ref, wi_ref, lb_ref, sin_ref, o_ref, sout_ref,
               q_s, k_s, lf_s, v_s, o_s, state, *, rows):
    d = pl.program_id(1)
    step = pl.program_id(2)
    nsteps = pl.num_programs(2)

    @pl.when(step == 0)
    def _():
        state[...] = sin_ref[...]

    rr = lax.broadcasted_iota(I32, (rows, rows), 0)
    cc = lax.broadcasted_iota(I32, (rows, rows), 1)
    perm = (cc == jnp.where(d == 0, rr, rows - 1 - rr)).astype(BF16)
    hb = _dot(perm, h_ref[...]).astype(BF16)
    q_s[...] = _silu(_dot(hb, wq_ref[...]))
    fz = _dot(hb, wf_ref[...])
    t = jnp.exp(-jnp.abs(fz))
    r = 1.0 / (1.0 + t)
    big, small = r, t * r
    pos = fz >= 0.0
    lb = lb_ref[...]
    lf_s[...] = jnp.log(lb + (1.0 - lb) * jnp.where(pos, big, small))
    k_s[...] = (1.0 - lb) * jnp.where(pos, small, big)
    v_s[...] = _dot(hb, wi_ref[...])

    c = HG_CHUNK
    r64 = lax.broadcasted_iota(I32, (c, c), 0)
    c64 = lax.broadcasted_iota(I32, (c, c), 1)
    tril = (r64 >= c64).astype(F32)

    def head(hh, carry):
        off = pl.multiple_of(hh * HG_DIM, HG_DIM)
        s_t = state[hh]
        for ch in range(rows // c):
            rs = slice(ch * c, (ch + 1) * c)
            o, s_t = _hg_chunk(q_s[rs, pl.ds(off, HG_DIM)], k_s[rs, pl.ds(off, HG_DIM)],
                               lf_s[rs, pl.ds(off, HG_DIM)], v_s[rs, pl.ds(off, HG_DIM)],
                               s_t, tril, c64, r64)
            o_s[rs, pl.ds(off, HG_DIM)] = o
        state[hh] = s_t
        return carry

    lax.fori_loop(0, HG_HEADS, head, 0)
    o_ref[...] = _dot(perm, o_s[...].astype(BF16)).astype(BF16)

    @pl.when(step == nsteps - 1)
    def _():
        sout_ref[...] = state[...]


def _hg_scan(h_view, rows, nsteps, h_index, wq, wf, wi, lb, s_in):
    nb = h_view.shape[0]

    def pos(d, s):
        return jnp.where(d == 0, s, nsteps - 1 - s)

    h_spec = pl.BlockSpec((None, rows, D_MODEL), lambda b, d, s: (b,) + h_index(pos(d, s)))
    o_spec = pl.BlockSpec((None, None, rows, D_MODEL), lambda b, d, s: (d, b) + h_index(pos(d, s)))
    st_spec = pl.BlockSpec((None, None, HG_HEADS, HG_DIM, HG_DIM), lambda b, d, s: (b, d, 0, 0, 0))
    return pl.pallas_call(
        functools.partial(_hg_kernel, rows=rows),
        grid=(nb, 2, nsteps),
        in_specs=[h_spec,
                  pl.BlockSpec(wq.shape, lambda b, d, s: (0, 0)),
                  pl.BlockSpec((None, D_MODEL, HG_KEY), lambda b, d, s: (d, 0, 0)),
                  pl.BlockSpec(wi.shape, lambda b, d, s: (0, 0)),
                  pl.BlockSpec((None, 1, HG_KEY), lambda b, d, s: (d, 0, 0)),
                  st_spec],
        out_specs=[o_spec, st_spec],
        out_shape=[jax.ShapeDtypeStruct((2,) + h_view.shape, BF16),
                   jax.ShapeDtypeStruct(s_in.shape, F32)],
        scratch_shapes=[pltpu.VMEM((rows, HG_KEY), F32)] * 5
        + [pltpu.VMEM((HG_HEADS, HG_DIM, HG_DIM), F32)],
        compiler_params=_cparams(3),
        name="hgrn2_scan",
    )(h_view, wq, wf, wi, lb, s_in)


def _proj_kernel(h_ref, w_ref, o_ref, *, col_tile):
    h = h_ref[...]
    for c in range(w_ref.shape[1] // col_tile):
        sl = slice(c * col_tile, (c + 1) * col_tile)
        o_ref[:, sl] = _dot(h, w_ref[:, sl]).astype(BF16)


def _gate_proj(h_lat, w):
    nb, n_lat, d = h_lat.shape
    n = w.shape[1]
    return pl.pallas_call(
        functools.partial(_proj_kernel, col_tile=512),
        grid=(nb, n_lat // TOK_TILE),
        in_specs=[pl.BlockSpec((None, TOK_TILE, d), lambda b, i: (b, i, 0)),
                  pl.BlockSpec(w.shape, lambda b, i: (0, 0))],
        out_specs=pl.BlockSpec((None, TOK_TILE, n), lambda b, i: (b, i, 0)),
        out_shape=jax.ShapeDtypeStruct((nb, n_lat, n), BF16),
        compiler_params=_cparams(2),
        name="gate_proj",
    )(h_lat, w)


def _merge_kernel(x_ref, yf_ref, yb_ref, of_ref, ob_ref, zg_ref, wbs_ref, wbh_ref, wo_ref,
                  gs_ref, gh_ref, gpm_ref, gpf_ref, mod_ref, wr_ref, br_ref,
                  xl_ref, h2_ref, eid_ref, ew_ref):
    b = pl.program_id(0)
    tm = x_ref.shape[0]
    d = D_MODEL
    z = zg_ref[:, 0:SSM_D_INNER].astype(F32)
    yz = (yf_ref[...].astype(F32) + yb_ref[...].astype(F32)) * _silu(z)
    parts = []
    for g in range(SSM_GROUPS):
        gsl = slice(g * SSM_GROUP_WIDTH, (g + 1) * SSM_GROUP_WIDTH)
        parts.append(_rms(yz[:, gsl], gs_ref[:, gsl]).astype(BF16))
    br_ssm = _dot(jnp.concatenate(parts, axis=1), wbs_ref[...])

    o = of_ref[...].astype(F32) + ob_ref[...].astype(F32)
    og = zg_ref[:, SSM_D_INNER:SSM_D_INNER + d].astype(F32)
    parts = []
    for hh in range(HG_HEADS):
        hsl = slice(hh * HG_DIM, (hh + 1) * HG_DIM)
        parts.append((_rms(o[:, hsl], gh_ref[...]) * _silu(og[:, hsl])).astype(BF16))
    br_hg = _dot(jnp.concatenate(parts, axis=1), wbh_ref[...])

    gm = zg_ref[:, SSM_D_INNER + d:SSM_D_INNER + 2 * d].astype(F32)
    gh = zg_ref[:, SSM_D_INNER + 2 * d:SSM_D_INNER + 3 * d].astype(F32)
    mixed = _sigmoid(gm) * br_ssm + _sigmoid(gh) * br_hg
    mix = _dot(mixed.astype(BF16), wo_ref[...])

    gt1 = mod_ref[pl.ds(b, 1), pl.ds(2 * d, d)]
    sh2 = mod_ref[pl.ds(b, 1), pl.ds(3 * d, d)]
    sc2 = mod_ref[pl.ds(b, 1), pl.ds(4 * d, d)]
    xl = x_ref[...] + gt1 * _rms(mix, gpm_ref[...])
    xl_ref[...] = xl
    h2 = _rms(xl, gpf_ref[...]) * (1.0 + sc2) + sh2
    h2_ref[...] = h2

    logits = jnp.dot(h2, wr_ref[...], precision=HIGHEST, preferred_element_type=F32) + br_ref[...]
    lane = lax.broadcasted_iota(I32, (tm, LANES), 1)
    lane_f = lane.astype(F32)
    neg = jnp.float32(-jnp.inf)
    far = jnp.float32(2 * LANES)
    gl = jnp.where(lane < MOE_GROUPS, logits, neg)
    gmax = jnp.max(gl, axis=1, keepdims=True)
    p_grp = 1.0 / jnp.sum(jnp.exp(gl - gmax), axis=1, keepdims=True)
    grp = jnp.min(jnp.where(gl == gmax, lane_f, far), axis=1, keepdims=True)
    lo = MOE_GROUPS + MOE_EXPERTS_PER_GROUP * grp
    el = jnp.where(jnp.logical_and(lane_f >= lo, lane_f < lo + MOE_EXPERTS_PER_GROUP), logits, neg)
    m1 = jnp.max(el, axis=1, keepdims=True)
    i1 = jnp.min(jnp.where(el == m1, lane_f, far), axis=1, keepdims=True)
    el2 = jnp.where(lane_f == i1, neg, el)
    m2 = jnp.max(el2, axis=1, keepdims=True)
    i2 = jnp.min(jnp.where(el2 == m2, lane_f, far), axis=1, keepdims=True)
    e2 = jnp.exp(m2 - m1)
    w1 = p_grp / (1.0 + e2)
    w2 = p_grp * e2 / (1.0 + e2)
    eid = jnp.where(lane == 0, i1, i2) - float(MOE_GROUPS)
    eid_ref[...] = jnp.where(lane < 2, eid, 0.0).astype(I32)
    ew_ref[...] = jnp.where(lane == 0, w1, jnp.where(lane == 1, w2, 0.0))


def _merge(x, y, o, zg, w_bs, w_bh, w_o, g_ssm, g_hg, g_pm, g_pf, mod, w_r, b_r):
    nb, n_lat, d = x.shape
    tm = TOK_TILE
    ctx_tiles = (y.shape[2] - n_lat) // tm

    def full(a):
        return pl.BlockSpec(a.shape, lambda b, i: (0,) * a.ndim)

    tok = lambda w: pl.BlockSpec((None, tm, w), lambda b, i: (b, i, 0))
    return pl.pallas_call(
        _merge_kernel,
        grid=(nb, n_lat // tm),
        in_specs=[tok(d),
                  pl.BlockSpec((None, None, tm, SSM_D_INNER), lambda b, i: (0, b, i + ctx_tiles, 0)),
                  pl.BlockSpec((None, None, tm, SSM_D_INNER), lambda b, i: (1, b, i + ctx_tiles, 0)),
                  pl.BlockSpec((None, None, tm, d), lambda b, i: (0, b, i, 0)),
                  pl.BlockSpec((None, None, tm, d), lambda b, i: (1, b, i, 0)),
                  tok(zg.shape[2]),
                  full(w_bs), full(w_bh), full(w_o), full(g_ssm), full(g_hg), full(g_pm), full(g_pf),
                  full(mod), full(w_r), full(b_r)],
        out_specs=[tok(d), tok(d), tok(LANES), tok(LANES)],
        out_shape=[jax.ShapeDtypeStruct((nb, n_lat, d), F32),
                   jax.ShapeDtypeStruct((nb, n_lat, d), F32),
                   jax.ShapeDtypeStruct((nb, n_lat, LANES), I32),
                   jax.ShapeDtypeStruct((nb, n_lat, LANES), F32)],
        compiler_params=_cparams(2),
        name="merge_route",
    )(x, y, y, o, o, zg, w_bs, w_bh, w_o, g_ssm, g_hg, g_pm, g_pf, mod, w_r, b_r)


def _rank_kernel(eid_ref, dest_ref, blk_ref, cnt, base):
    p = pl.program_id(0)
    i = pl.program_id(1)
    t = eid_ref.shape[0]
    lane = lax.broadcasted_iota(I32, (t, LANES), 1)
    e = eid_ref[...]
    oh0 = (lane == e[:, 0:1]).astype(F32)
    oh1 = (lane == e[:, 1:2]).astype(F32)
    c0 = jnp.sum(oh0, axis=0, keepdims=True)
    c1 = jnp.sum(oh1, axis=0, keepdims=True)

    @pl.when(jnp.logical_and(p == 0, i == 0))
    def _():
        cnt[...] = jnp.zeros(cnt.shape, F32)

    @pl.when(p == 0)
    def _():
        cnt[...] = cnt[...] + c0 + c1

    @pl.when(jnp.logical_and(p == 1, i == 0))
    def _():
        lane1 = lax.broadcasted_iota(I32, (SUBLANES, LANES), 1)
        padded = jnp.floor((cnt[...] + (MOE_ROWS - 1)) / MOE_ROWS) * MOE_ROWS
        padded = jnp.broadcast_to(jnp.where(lane1[0:1] < MOE_EXPERTS, padded, 0.0), (SUBLANES, LANES))
        rr = lax.broadcasted_iota(I32, (LANES, LANES), 0)
        cc = lax.broadcasted_iota(I32, (LANES, LANES), 1)
        start = jnp.dot(padded, (rr < cc).astype(F32), precision=HIGHEST, preferred_element_type=F32)
        base[...] = start[0:1]
        pad_end = start[0:1] + padded[0:1]
        pad_end = jnp.where(lane1[0:1] < MOE_EXPERTS, pad_end, jnp.float32(2.0 ** 30))
        nblk = blk_ref.shape[0]
        bstart = lax.broadcasted_iota(I32, (nblk, LANES), 0).astype(F32) * MOE_ROWS
        owner = jnp.sum((pad_end <= bstart).astype(F32), axis=1, keepdims=True)
        owner = jnp.minimum(owner, MOE_EXPERTS - 1)
        total = jnp.max(jnp.where(lane1[0:1] == MOE_EXPERTS - 1, pad_end, 0.0), axis=1, keepdims=True)
        lane_b = lax.broadcasted_iota(I32, (nblk, LANES), 1)
        blk_ref[...] = jnp.where(lane_b == 0, owner, total / MOE_ROWS).astype(I32)

    @pl.when(p == 1)
    def _():
        rr = lax.broadcasted_iota(I32, (t, t), 0)
        cc = lax.broadcasted_iota(I32, (t, t), 1)
        before = (rr > cc).astype(BF16)
        b0 = base[...]
        r0 = _dot(before, oh0.astype(BF16)) + b0
        r1 = _dot(before, oh1.astype(BF16)) + b0 + c0
        d0 = jnp.sum(oh0 * r0, axis=1, keepdims=True)
        d1 = jnp.sum(oh1 * r1, axis=1, keepdims=True)
        dest_ref[...] = jnp.where(lane == 0, d0, jnp.where(lane == 1, d1, 0.0)).astype(I32)
        base[...] = b0 + c0 + c1


def _rank(eid, n_blocks):
    n = eid.shape[0]
    t = RANK_TILE
    nblk_pad = -(-n_blocks // SUBLANES) * SUBLANES
    return pl.pallas_call(
        _rank_kernel,
        grid=(2, n // t),
        in_specs=[pl.BlockSpec((t, LANES), lambda p, i: (i, 0))],
        out_specs=[pl.BlockSpec((t, LANES), lambda p, i: (p * i, 0)),
                   pl.BlockSpec((nblk_pad, LANES), lambda p, i: (0, 0))],
        out_shape=[jax.ShapeDtypeStruct((n, LANES), I32),
                   jax.ShapeDtypeStruct((nblk_pad, LANES), I32)],
        scratch_shapes=[pltpu.VMEM((1, LANES), F32), pltpu.VMEM((1, LANES), F32)],
        compiler_params=_cparams(2),
        name="moe_rank",
    )(eid)


def _row_copy(src, s, dst, t, sem):
    return pltpu.make_async_copy(src.at[pl.ds(s, 1)], dst.at[pl.ds(t, 1)], sem)


def _dispatch_kernel(dest_ref, h_ref, zin_ref, out_ref, sem):
    del zin_ref
    i = pl.program_id(0)
    t = h_ref.shape[0]

    def start(r, carry):
        base = (i * t + r) * 2
        _row_copy(h_ref, r, out_ref, dest_ref[base], sem).start()
        _row_copy(h_ref, r, out_ref, dest_ref[base + 1], sem).start()
        return carry

    lax.fori_loop(0, t, start, 0)

    def wait(r, carry):
        _row_copy(h_ref, 0, out_ref, 0, sem).wait()
        _row_copy(h_ref, 0, out_ref, 0, sem).wait()
        return carry

    lax.fori_loop(0, t, wait, 0)


def _dispatch(dest_flat, h2, cap):
    n, d = h2.shape
    t = TOK_TILE
    return pl.pallas_call(
        _dispatch_kernel,
        grid_spec=pltpu.PrefetchScalarGridSpec(
            num_scalar_prefetch=1,
            grid=(n // t,),
            in_specs=[pl.BlockSpec((t, d), lambda i, dest: (i, 0)),
                      pl.BlockSpec(memory_space=pl.ANY)],
            out_specs=pl.BlockSpec(memory_space=pl.ANY),
            scratch_shapes=[pltpu.SemaphoreType.DMA]),
        out_shape=jax.ShapeDtypeStruct((cap, d), F32),
        input_output_aliases={2: 0},
        compiler_params=_cparams(1),
        name="moe_dispatch",
    )(dest_flat, h2, jnp.zeros((cap, d), F32))


def _expert_kernel(be_ref, na_ref, x_ref, wg_ref, wu_ref, wd_ref, y_ref, wg_s, wu_s, wd_s):
    i = pl.program_id(0)
    e = be_ref[i]
    prev = be_ref[jnp.maximum(i - 1, 0)]

    @pl.when(jnp.logical_or(i == 0, e != prev))
    def _():
        wg_s[...] = wg_ref[...].astype(BF16)
        wu_s[...] = wu_ref[...].astype(BF16)
        wd_s[...] = wd_ref[...].astype(BF16)

    @pl.when(i < na_ref[0])
    def _():
        x = x_ref[...].astype(BF16)
        hid = _silu(_dot(x, wg_s[...])) * _dot(x, wu_s[...])
        y_ref[...] = _dot(hid.astype(BF16), wd_s[...])

    @pl.when(i >= na_ref[0])
    def _():
        y_ref[...] = jnp.zeros(y_ref.shape, F32)


def _experts(blk_e, n_act, xs, w_gate, w_up, w_down):
    cap, d = xs.shape
    r = MOE_ROWS
    ff = w_gate.shape[2]
    return pl.pallas_call(
        _expert_kernel,
        grid_spec=pltpu.PrefetchScalarGridSpec(
            num_scalar_prefetch=2,
            grid=(cap // r,),
            in_specs=[pl.BlockSpec((r, d), lambda i, be, na: (i, 0)),
                      pl.BlockSpec((None, d, ff), lambda i, be, na: (be[i], 0, 0)),
                      pl.BlockSpec((None, d, ff), lambda i, be, na: (be[i], 0, 0)),
                      pl.BlockSpec((None, ff, d), lambda i, be, na: (be[i], 0, 0))],
            out_specs=pl.BlockSpec((r, d), lambda i, be, na: (i, 0)),
            scratch_shapes=[pltpu.VMEM((d, ff), BF16), pltpu.VMEM((d, ff), BF16),
                            pltpu.VMEM((ff, d), BF16)]),
        out_shape=jax.ShapeDtypeStruct((cap, d), F32),
        compiler_params=_cparams(1),
        name="moe_experts",
    )(blk_e, n_act, xs, w_gate, w_up, w_down)


def _combine_kernel(dest_ref, ys_ref, xl_ref, ew_ref, mod_ref, g_ref, out_ref, buf, sem, *, tiles_per_batch):
    i = pl.program_id(0)
    t = xl_ref.shape[0]

    def start(r, carry):
        base = (i * t + r) * 2
        _row_copy(ys_ref, dest_ref[base], buf.at[0], r, sem).start()
        _row_copy(ys_ref, dest_ref[base + 1], buf.at[1], r, sem).start()
        return carry

    lax.fori_loop(0, t, start, 0)

    def wait(r, carry):
        _row_copy(ys_ref, 0, buf.at[0], 0, sem).wait()
        _row_copy(ys_ref, 0, buf.at[1], 0, sem).wait()
        return carry

    lax.fori_loop(0, t, wait, 0)
    b = i // tiles_per_batch
    ew = ew_ref[...]
    ffn = buf[0] * ew[:, 0:1] + buf[1] * ew[:, 1:2]
    gt2 = mod_ref[pl.ds(b, 1), pl.ds(5 * D_MODEL, D_MODEL)]
    out_ref[...] = xl_ref[...] + gt2 * _rms(ffn, g_ref[...])


def _combine(dest_flat, ys, xl, ew, mod, g, tiles_per_batch):
    n, d = xl.shape
    t = TOK_TILE
    return pl.pallas_call(
        functools.partial(_combine_kernel, tiles_per_batch=tiles_per_batch),
        grid_spec=pltpu.PrefetchScalarGridSpec(
            num_scalar_prefetch=1,
            grid=(n // t,),
            in_specs=[pl.BlockSpec(memory_space=pl.ANY),
                      pl.BlockSpec((t, d), lambda i, dest: (i, 0)),
                      pl.BlockSpec((t, LANES), lambda i, dest: (i, 0)),
                      pl.BlockSpec(mod.shape, lambda i, dest: (0, 0)),
                      pl.BlockSpec(g.shape, lambda i, dest: (0, 0))],
            out_specs=pl.BlockSpec((t, d), lambda i, dest: (i, 0)),
            scratch_shapes=[pltpu.VMEM((2, t, d), F32), pltpu.SemaphoreType.DMA]),
        out_shape=jax.ShapeDtypeStruct((n, d), F32),
        compiler_params=_cparams(1),
        name="moe_combine",
    )(dest_flat, ys, xl, ew, mod, g)


def kernel(x, c, ctx, c_ctx, w_ada, b_ada, g_pre_mix, g_post_mix, g_pre_ffn, g_post_ffn, w_in,
           conv_w, conv_b, dt_bias, a_log, d_skip, g_ssm_norm, hg_lb, g_hg_norm, w_branch_ssm,
           w_branch_hg, w_out, w_group_router, b_group_router, w_expert_router, b_expert_router,
           w_gate, w_up, w_down):
    nb, n_lat, d = x.shape
    n_ctx = ctx.shape[1]
    rows = n_lat // GRID_W
    assert w_ada.shape[0] == 1 and d == D_MODEL and nb + 1 <= SUBLANES
    assert rows % HG_CHUNK == 0 and n_ctx % (2 * HG_CHUNK) == 0

    w = w_in[0]
    o0 = 0
    w_z = w[:, o0:o0 + SSM_D_INNER]; o0 += SSM_D_INNER
    w_xbc = w[:, o0:o0 + SSM_CONV_DIM].astype(BF16); o0 += SSM_CONV_DIM
    w_dtr = w[:, o0:o0 + 2 * SSM_HEADS]; o0 += 2 * SSM_HEADS
    w_q = w[:, o0:o0 + HG_KEY].astype(BF16); o0 += HG_KEY
    w_f = jnp.stack([w[:, o0:o0 + HG_KEY], w[:, o0 + HG_KEY:o0 + 2 * HG_KEY]]).astype(BF16); o0 += 2 * HG_KEY
    w_i = w[:, o0:o0 + HG_KEY].astype(BF16); o0 += HG_KEY
    w_zg = jnp.concatenate([w_z, w[:, o0:o0 + 3 * d]], axis=1).astype(BF16)
    pad = jnp.zeros((d, LANES - SSM_HEADS), F32)
    w_dt = jnp.concatenate([w_dtr[:, :SSM_HEADS], pad, w_dtr[:, SSM_HEADS:], pad], axis=1).astype(BF16)
    zpad = jnp.zeros((LANES - SSM_HEADS,), F32)
    dtb = jnp.concatenate([dt_bias[0, 0], zpad, dt_bias[0, 1], zpad]).reshape(1, 2 * LANES)
    a_log_p = jnp.pad(a_log[0], ((0, 0), (0, LANES - SSM_HEADS))).reshape(2, 1, LANES)
    dskip_e = jnp.repeat(d_skip[0], SSM_HEAD_DIM, axis=-1).reshape(2, 1, SSM_D_INNER)
    head_of_lane = jnp.arange(SSM_D_INNER, dtype=I32) // SSM_HEAD_DIM
    expand = (jnp.arange(LANES, dtype=I32)[:, None] == head_of_lane[None, :]).astype(BF16)
    lb = jax.nn.softmax(hg_lb.astype(F32), axis=0)[0].reshape(2, 1, HG_KEY)
    w_r = jnp.concatenate([w_group_router[0],
                           jnp.transpose(w_expert_router[0], (1, 0, 2)).reshape(d, MOE_EXPERTS),
                           jnp.zeros((d, LANES - MOE_GROUPS - MOE_EXPERTS), F32)], axis=1)
    b_r = jnp.concatenate([b_group_router[0], b_expert_router[0].reshape(-1),
                           jnp.zeros((LANES - MOE_GROUPS - MOE_EXPERTS,), F32)]).reshape(1, LANES)
    g_hg = g_hg_norm[0].reshape(1, HG_DIM)

    cc = jnp.concatenate([c, c_ctx[None], jnp.zeros((SUBLANES - nb - 1, d), F32)], axis=0)
    mod = _modulation(cc, w_ada[0], b_ada[0])
    h_all, h_lat = _pre_norm(x, ctx, mod, g_pre_mix)

    xbc, dt = _ssd_inputs(h_all, w_xbc, w_dt, conv_w[0], conv_b, dtb)
    y = _ssd_scan(xbc, dt, a_log_p, dskip_e, expand)

    s0 = jnp.zeros((nb, 2, HG_HEADS, HG_DIM, HG_DIM), F32)
    _, s_ctx = _hg_scan(h_all[:, :n_ctx], 2 * HG_CHUNK, n_ctx // (2 * HG_CHUNK), lambda p: (p, 0),
                        w_q, w_f, w_i, lb, s0)
    o_cols, _ = _hg_scan(h_lat.reshape(nb, rows, GRID_W * d), rows, GRID_W, lambda p: (0, p),
                         w_q, w_f, w_i, lb, s_ctx)
    o = o_cols.reshape(2, nb, n_lat, d)

    zg = _gate_proj(h_lat, w_zg)
    xl, h2, eid, ew = _merge(x, y, o, zg, w_branch_ssm[0].astype(BF16), w_branch_hg[0].astype(BF16),
                             w_out[0].astype(BF16), g_ssm_norm, g_hg, g_post_mix, g_pre_ffn, mod,
                             w_r, b_r)

    n = nb * n_lat
    cap = -(-(2 * n + MOE_EXPERTS * MOE_ROWS) // MOE_ROWS) * MOE_ROWS
    n_blocks = cap // MOE_ROWS
    dest, blk = _rank(eid.reshape(n, LANES), n_blocks)
    dest_flat = dest[:, 0:2].reshape(-1)
    xs = _dispatch(dest_flat, h2.reshape(n, d), cap)
    ys = _experts(blk[:n_blocks, 0], blk[0:1, 1], xs, w_gate[0], w_up[0], w_down[0])
    out = _combine(dest_flat, ys, xl.reshape(n, d), ew.reshape(n, LANES), mod, g_post_ffn,
                   n_lat // TOK_TILE)
    return out.reshape(nb, n_lat, d)
```

```python
import functools

import jax
import jax.numpy as jnp
from jax import lax
from jax.experimental import pallas as pl
from jax.experimental.pallas import tpu as pltpu

F32 = jnp.float32
BF16 = jnp.bfloat16
I32 = jnp.int32
HIGHEST = lax.Precision.HIGHEST

D_MODEL = 1024
GRID_W = 64
RMS_EPS = 1e-6
SSM_D_INNER = 2048
SSM_HEAD_DIM = 64
SSM_HEADS = 32
SSM_GROUPS = 4
SSM_HEADS_PER_GROUP = 8
SSM_STATE = 128
SSM_GROUP_WIDTH = SSM_HEADS_PER_GROUP * SSM_HEAD_DIM
SSM_CONV_DIM = SSM_D_INNER + 2 * SSM_GROUPS * SSM_STATE
HG_HEADS = 8
HG_DIM = 128
HG_KEY = HG_HEADS * HG_DIM
MOE_GROUPS = 4
MOE_EXPERTS_PER_GROUP = 8
MOE_EXPERTS = 32
MOE_D_FF = 512

LANES = 128
SUBLANES = 8
VMEM_LIMIT_BYTES = 56 * 1024 * 1024

TOK_TILE = 256
SSD_CHUNK = 128
HG_CHUNK = 64
HG_SUB = 16
HG_COLS_PER_STEP = 8
HG_COLS_PER_PASS = 4
MOE_ROWS = 256
RANK_TILE = 512


def _cparams(n_axes):
    return pltpu.CompilerParams(dimension_semantics=("arbitrary",) * n_axes,
                                vmem_limit_bytes=VMEM_LIMIT_BYTES)


def _sigmoid(x):
    return 1.0 / (1.0 + jnp.exp(-x))


def _silu(x):
    return x * _sigmoid(x)


def _rms(v, g):
    ms = jnp.mean(v * v, axis=-1, keepdims=True)
    return v * lax.rsqrt(ms + RMS_EPS) * g


def _dot(a, b):
    return jnp.dot(a, b, preferred_element_type=F32)


def _dot_nt(a, b):
    return lax.dot_general(a, b, (((1,), (1,)), ((), ())), preferred_element_type=F32)


def _dot_tn(a, b):
    return lax.dot_general(a, b, (((0,), (0,)), ((), ())), preferred_element_type=F32)


def _mod_kernel(c_ref, w_ref, b_ref, o_ref):
    s = _silu(c_ref[...])
    o_ref[...] = jnp.dot(s, w_ref[...], precision=HIGHEST, preferred_element_type=F32) + b_ref[...]


def _modulation(cc, w_ada, b_ada):
    n = w_ada.shape[1]
    tn = 1024
    return pl.pallas_call(
        _mod_kernel,
        grid=(n // tn,),
        in_specs=[pl.BlockSpec((SUBLANES, D_MODEL), lambda j: (0, 0)),
                  pl.BlockSpec((D_MODEL, tn), lambda j: (0, j)),
                  pl.BlockSpec((1, tn), lambda j: (0, j))],
        out_specs=pl.BlockSpec((SUBLANES, tn), lambda j: (0, j)),
        out_shape=jax.ShapeDtypeStruct((SUBLANES, n), F32),
        compiler_params=_cparams(1),
        name="adaln_mod",
    )(cc, w_ada, b_ada.reshape(1, n))


def _norm_kernel(x_ref, ctx_ref, mod_ref, g_ref, all_ref, lat_ref, *, ctx_row):
    b = pl.program_id(0)
    i = pl.program_id(1)
    g = g_ref[...]

    def nm(v, row):
        sh = mod_ref[pl.ds(row, 1), pl.ds(0, D_MODEL)]
        sc = mod_ref[pl.ds(row, 1), pl.ds(D_MODEL, D_MODEL)]
        return _rms(v, g) * (1.0 + sc) + sh

    @pl.when(i == 0)
    def _():
        all_ref[...] = nm(ctx_ref[...], ctx_row).astype(BF16)

    @pl.when(i > 0)
    def _():
        h = nm(x_ref[...], b)
        all_ref[...] = h.astype(BF16)
        lat_ref[...] = h


def _pre_norm(x, ctx, mod, g):
    nb, n_lat, d = x.shape
    n_ctx = ctx.shape[1]
    assert n_ctx == TOK_TILE and n_lat % TOK_TILE == 0
    nt = n_lat // TOK_TILE
    return pl.pallas_call(
        functools.partial(_norm_kernel, ctx_row=nb),
        grid=(nb, nt + 1),
        in_specs=[pl.BlockSpec((None, TOK_TILE, d), lambda b, i: (b, jnp.maximum(i - 1, 0), 0)),
                  pl.BlockSpec((None, TOK_TILE, d), lambda b, i: (b, 0, 0)),
                  pl.BlockSpec(mod.shape, lambda b, i: (0, 0)),
                  pl.BlockSpec((1, d), lambda b, i: (0, 0))],
        out_specs=[pl.BlockSpec((None, TOK_TILE, d), lambda b, i: (b, i, 0)),
                   pl.BlockSpec((None, TOK_TILE, d), lambda b, i: (b, jnp.maximum(i - 1, 0), 0))],
        out_shape=[jax.ShapeDtypeStruct((nb, n_ctx + n_lat, d), BF16),
                   jax.ShapeDtypeStruct((nb, n_lat, d), F32)],
        compiler_params=_cparams(2),
        name="pre_norm",
    )(x, ctx, mod, g)


XBC_COL_TILE = 512


def _xbc_kernel(h_ref, hp_ref, hn_ref, w_ref, wdt_ref, cw_ref, cb_ref, dtb_ref, xbc_ref, dt_ref,
                *, n_tiles):
    i = pl.program_id(1)
    h = h_ref[...]
    hp = hp_ref[...]
    hn = hn_ref[...]
    tm = h.shape[0]
    prev_ok = (i >= 2).astype(F32)
    next_ok = jnp.logical_and(i >= 1, i < n_tiles - 1).astype(F32)
    row = lax.broadcasted_iota(I32, (tm, XBC_COL_TILE), 0)
    for c in range(SSM_CONV_DIM // XBC_COL_TILE):
        sl = slice(c * XBC_COL_TILE, (c + 1) * XBC_COL_TILE)
        w = w_ref[:, sl]
        raw = _dot(h, w)
        rp = _dot(hp, w)[SUBLANES - 1:SUBLANES] * prev_ok
        rn = _dot(hn, w)[0:1] * next_ok
        up = jnp.where(row == 0, rp, pltpu.roll(raw, 1, 0))
        dn = jnp.where(row == tm - 1, rn, pltpu.roll(raw, tm - 1, 0))
        conv = cw_ref[0:1, sl] * up + cw_ref[1:2, sl] * raw + cw_ref[2:3, sl] * dn + cb_ref[:, sl]
        xbc_ref[:, sl] = _silu(conv).astype(BF16)
    u = _dot(h, wdt_ref[...]) + dtb_ref[...]
    dt_ref[...] = jnp.maximum(u, 0.0) + jnp.log(1.0 + jnp.exp(-jnp.abs(u)))


def _ssd_inputs(h_all, w_xbc, w_dt, conv_w, conv_b, dt_bias):
    nb, ltot, d = h_all.shape
    nt = ltot // TOK_TILE
    n8 = ltot // SUBLANES
    per = TOK_TILE // SUBLANES
    return pl.pallas_call(
        functools.partial(_xbc_kernel, n_tiles=nt),
        grid=(nb, nt),
        in_specs=[pl.BlockSpec((None, TOK_TILE, d), lambda b, i: (b, i, 0)),
                  pl.BlockSpec((None, SUBLANES, d), lambda b, i: (b, jnp.maximum(i * per - 1, 0), 0)),
                  pl.BlockSpec((None, SUBLANES, d), lambda b, i: (b, jnp.minimum((i + 1) * per, n8 - 1), 0)),
                  pl.BlockSpec(w_xbc.shape, lambda b, i: (0, 0)),
                  pl.BlockSpec(w_dt.shape, lambda b, i: (0, 0)),
                  pl.BlockSpec(conv_w.shape, lambda b, i: (0, 0)),
                  pl.BlockSpec(conv_b.shape, lambda b, i: (0, 0)),
                  pl.BlockSpec(dt_bias.shape, lambda b, i: (0, 0))],
        out_specs=[pl.BlockSpec((None, TOK_TILE, SSM_CONV_DIM), lambda b, i: (b, i, 0)),
                   pl.BlockSpec((None, TOK_TILE, 2 * LANES), lambda b, i: (b, i, 0))],
        out_shape=[jax.ShapeDtypeStruct((nb, ltot, SSM_CONV_DIM), BF16),
                   jax.ShapeDtypeStruct((nb, ltot, 2 * LANES), F32)],
        compiler_params=_cparams(2),
        name="ssd_inputs",
    )(h_all, h_all, h_all, w_xbc, w_dt, conv_w, conv_b, dt_bias)


def _ssd_kernel(xbc_ref, dt_ref, alog_ref, dskip_ref, e_ref, y_ref, state):
    d = pl.program_id(1)
    step = pl.program_id(2)
    q = SSD_CHUNK

    @pl.when(step == 0)
    def _():
        state[...] = jnp.zeros(state.shape, F32)

    lane = lax.broadcasted_iota(I32, (1, LANES), 1)
    a = jnp.where(lane < SSM_HEADS, -jnp.exp(alog_ref[...]), 0.0)
    dt = dt_ref[...]
    adt = dt * a
    r = lax.broadcasted_iota(I32, (q, q), 0)
    c = lax.broadcasted_iota(I32, (q, q), 1)
    mask = jnp.where(d == 0, r - c, c - r) >= 0
    cs = jnp.dot(mask.astype(F32), adt, precision=HIGHEST, preferred_element_type=F32)
    tot = jnp.sum(adt, axis=0, keepdims=True)
    cs_t = cs.T
    lhs = jnp.concatenate([dt, jnp.exp(cs), jnp.exp(tot - cs),
                           jnp.broadcast_to(jnp.exp(tot), (SUBLANES, LANES))], axis=0).astype(BF16)
    lane_h = lax.broadcasted_iota(I32, (q, LANES), 1)
    for g in range(SSM_GROUPS):
        gs = slice(g * SSM_GROUP_WIDTH, (g + 1) * SSM_GROUP_WIDTH)
        ex = _dot(lhs, e_ref[:, gs])
        dt_e, e1_e, e2_e, etot_e = ex[0:q], ex[q:2 * q], ex[2 * q:3 * q], ex[3 * q:3 * q + 1]
        x = xbc_ref[:, gs].astype(F32)
        bg = xbc_ref[:, SSM_D_INNER + g * SSM_STATE:SSM_D_INNER + (g + 1) * SSM_STATE]
        cg = xbc_ref[:, SSM_D_INNER + (SSM_GROUPS + g) * SSM_STATE:
                     SSM_D_INNER + (SSM_GROUPS + g + 1) * SSM_STATE]
        xdt = x * dt_e
        cb = _dot_nt(cg, bg)
        s_in = state[g]
        y_off = _dot(cg, s_in.astype(BF16)) * e1_e
        state[g] = s_in * etot_e + _dot_tn(bg, (xdt * e2_e).astype(BF16))
        parts = []
        for m in range(SSM_HEADS_PER_GROUP // 2):
            xp = xdt[:, m * LANES:(m + 1) * LANES]
            acc = None
            for half in range(2):
                hd = g * SSM_HEADS_PER_GROUP + 2 * m + half
                col = cs[:, hd:hd + 1]
                rowv = cs_t[hd:hd + 1, :]
                decay = jnp.exp(jnp.minimum(col - rowv, 0.0))
                w = jnp.where(mask, cb * decay, 0.0).astype(BF16)
                sel = (lane_h < SSM_HEAD_DIM) if half == 0 else (lane_h >= SSM_HEAD_DIM)
                t = _dot(w, jnp.where(sel, xp, 0.0).astype(BF16))
                acc = t if acc is None else acc + t
            parts.append(acc)
        y = jnp.concatenate(parts, axis=1) + y_off + x * dskip_ref[:, gs]
        y_ref[:, gs] = y.astype(BF16)


def _ssd_scan(xbc, dt, a_log_p, dskip_e, expand):
    nb, ltot, _ = xbc.shape
    q = SSD_CHUNK
    nc = ltot // q
    n_ctx_chunks = TOK_TILE // q

    def chunk(d, s):
        rev = jnp.where(s < n_ctx_chunks, n_ctx_chunks - 1 - s, nc + n_ctx_chunks - 1 - s)
        return jnp.where(d == 0, s, rev)

    return pl.pallas_call(
        _ssd_kernel,
        grid=(nb, 2, nc),
        in_specs=[pl.BlockSpec((None, q, SSM_CONV_DIM), lambda b, d, s: (b, chunk(d, s), 0)),
                  pl.BlockSpec((None, q, LANES), lambda b, d, s: (b, chunk(d, s), d)),
                  pl.BlockSpec((None, 1, LANES), lambda b, d, s: (d, 0, 0)),
                  pl.BlockSpec((None, 1, SSM_D_INNER), lambda b, d, s: (d, 0, 0)),
                  pl.BlockSpec(expand.shape, lambda b, d, s: (0, 0))],
        out_specs=pl.BlockSpec((None, None, q, SSM_D_INNER), lambda b, d, s: (d, b, chunk(d, s), 0)),
        out_shape=jax.ShapeDtypeStruct((2, nb, ltot, SSM_D_INNER), BF16),
        scratch_shapes=[pltpu.VMEM((SSM_GROUPS, SSM_STATE, SSM_GROUP_WIDTH), F32)],
        compiler_params=_cparams(3),
        name="ssd_scan",
    )(xbc, dt, a_log_p, dskip_e, expand)


def _hg_intra(q_s, k_s, b_s, r0, off):
    c = HG_CHUNK
    cols = pl.ds(off, HG_DIM)
    lane8 = lax.broadcasted_iota(I32, (SUBLANES, c), 1)
    lane_sub = lax.broadcasted_iota(I32, (HG_SUB, c), 1)
    rows = []
    for i in range(c // HG_SUB):
        lo = r0 + i * HG_SUB
        bi = b_s[lo:lo + HG_SUB, cols]
        qi = q_s[lo:lo + HG_SUB, cols]
        top = jnp.zeros((SUBLANES, c), F32)
        bot = jnp.zeros((SUBLANES, c), F32)
        for j in range(HG_SUB):
            s = lo + j
            at = i * HG_SUB + j
            brow = b_s[s:s + 1, cols]
            krow = k_s[s:s + 1, cols]
            if j < SUBLANES:
                p = qi * krow * jnp.exp2(bi - brow)
                top = jnp.where(lane8 == at, jnp.sum(p[0:SUBLANES], axis=1, keepdims=True), top)
                bot = jnp.where(lane8 == at, jnp.sum(p[SUBLANES:], axis=1, keepdims=True), bot)
            else:
                p = qi[SUBLANES:] * krow * jnp.exp2(bi[SUBLANES:] - brow)
                bot = jnp.where(lane8 == at, jnp.sum(p, axis=1, keepdims=True), bot)
        a_row = jnp.concatenate([top, bot], axis=0)
        if i > 0:
            bref = b_s[lo - 1:lo, cols]
            qt = qi * jnp.exp2(bi - bref)
            kt = k_s[r0:lo, cols] * jnp.exp2(bref - b_s[r0:lo, cols])
            ktp = jnp.concatenate([kt, jnp.zeros((c - i * HG_SUB, HG_DIM), F32)], axis=0)
            a_off = _dot_nt(qt.astype(BF16), ktp.astype(BF16))
            a_row = jnp.where(lane_sub < i * HG_SUB, a_off, a_row)
        rows.append(a_row)
    a = jnp.concatenate(rows, axis=0)
    r64 = lax.broadcasted_iota(I32, (c, c), 0)
    c64 = lax.broadcasted_iota(I32, (c, c), 1)
    return jnp.where(r64 >= c64, a, 0.0)


def _hg_kernel(h_ref, wq_ref, wf_ref, wi_ref, lb_ref, sin_ref, o_ref, sout_ref,
               pin_s, pout_s, q_s, k_s, b_s, v_s, qe_s, kd_s, dec_s, o_s, state, *, log2_rows, log2_ncol,
               pass_rows):
    d = pl.program_id(1)
    step = pl.program_id(2)
    nsteps = pl.num_programs(2)
    t_all = 1 << (log2_rows + log2_ncol)
    c = HG_CHUNK
    nch = pass_rows // c

    def source_row(p):
        src = ((p & ((1 << log2_rows) - 1)) << log2_ncol) + (p >> log2_rows)
        return jnp.where(d == 0, src, t_all - 1 - src)

    @pl.when(step == 0)
    def _():
        state[...] = sin_ref[...]
        shape = (t_all, t_all)
        pin_s[...] = (lax.broadcasted_iota(I32, shape, 1)
                      == source_row(lax.broadcasted_iota(I32, shape, 0))).astype(BF16)
        pout_s[...] = (lax.broadcasted_iota(I32, shape, 0)
                       == source_row(lax.broadcasted_iota(I32, shape, 1))).astype(BF16)

    r64 = lax.broadcasted_iota(I32, (c, c), 0)
    c64 = lax.broadcasted_iota(I32, (c, c), 1)
    tril = (r64 >= c64).astype(BF16)
    h_rows =h_ref[...].reshape(t_all, HG_KEY).astype(BF16)

    for ps in range(t_all // pass_rows):
        prs = slice(ps * pass_rows, (ps + 1) * pass_rows)
        hb = _dot(pin_s[prs, :], h_rows).astype(BF16)
        q_s[...] = _silu(_dot(hb, wq_ref[...]))
        fz = _dot(hb, wf_ref[...])
        t = jnp.exp(-jnp.abs(fz))
        r = 1.0 / (1.0 + t)
        big, small = r, t * r
        pos = fz >= 0.0
        lb = lb_ref[...]
        lf2 = jnp.log2(lb + (1.0 - lb) * jnp.where(pos, big, small))
        k_s[...] = (1.0 - lb) * jnp.where(pos, small, big)
        v_s[...] = _dot(hb, wi_ref[...]).astype(BF16)
        for ch in range(nch):
            rs = slice(ch * c, (ch + 1) * c)
            x = lf2[rs]
            hi = x.astype(BF16)
            rem = x - hi.astype(F32)
            mid = rem.astype(BF16)
            low = (rem - mid.astype(F32)).astype(BF16)
            b = _dot(tril, hi) + _dot(tril, mid) + _dot(tril, low)
            b_s[rs, :] = b
            btot = b[c - 1:c]
            qe_s[rs, :] = (q_s[rs, :] * jnp.exp2(b)).astype(BF16)
            kd_s[rs, :] = (k_s[rs, :] * jnp.exp2(btot - b)).astype(BF16)
            dec_s[ch:ch + 1, :] = jnp.exp2(btot)

        def head(hh, carry):
            off = pl.multiple_of(hh * HG_DIM, HG_DIM)
            cols = pl.ds(off, HG_DIM)
            s_t = state[hh]
            for ch in range(nch):
                rs = slice(ch * c, (ch + 1) * c)
                a = _hg_intra(q_s, k_s, b_s, ch * c, off)
                v = v_s[rs, cols]
                o_s[rs, cols] = _dot_nt(qe_s[rs, cols], s_t.astype(BF16)) + _dot(a.astype(BF16), v)
                s_t = s_t * dec_s[ch:ch + 1, cols] + _dot_tn(v, kd_s[rs, cols])
            state[hh] = s_t
            return carry

        lax.fori_loop(0, HG_HEADS, head, 0)
        back = _dot(pout_s[:, prs], o_s[...].astype(BF16)).reshape(o_ref.shape)
        if ps == 0:
            o_ref[...] = back
        else:
            o_ref[...] += back

    @pl.when(step == nsteps - 1)
    def _():
        sout_ref[...] = state[...]


def _hg_scan(h_grid, ncol, wq, wf, wi, lb, s_in):
    if h_grid.ndim == 3:
        nb, rows, d_model = h_grid.shape
        ncol_total = 1
    else:
        nb, rows, ncol_total, d_model = h_grid.shape
    nsteps = ncol_total // ncol
    t_all = rows * ncol
    pass_rows = min(t_all, HG_COLS_PER_PASS * LANES)
    log2_rows, log2_ncol = rows.bit_length() - 1, ncol.bit_length() - 1
    assert rows == 1 << log2_rows and ncol == 1 << log2_ncol and t_all % pass_rows == 0
    assert pass_rows // HG_CHUNK <= SUBLANES

    def pos(d, s):
        return jnp.where(d == 0, s, nsteps - 1 - s)

    once = pl.Buffered(1)
    if h_grid.ndim == 3:
        h_spec = pl.BlockSpec((None, rows, d_model), lambda b, d, s: (b, 0, 0))
        o_spec = pl.BlockSpec((None, None, rows, d_model), lambda b, d, s: (d, b, 0, 0))
    else:
        h_spec = pl.BlockSpec((None, rows, ncol, d_model), lambda b, d, s: (b, 0, pos(d, s), 0))
        o_spec = pl.BlockSpec((None, None, rows, ncol, d_model), lambda b, d, s: (d, b, 0, pos(d, s), 0))
    st_spec = pl.BlockSpec((None, None, HG_HEADS, HG_DIM, HG_DIM), lambda b, d, s: (b, d, 0, 0, 0))
    wide_f32 = pltpu.VMEM((pass_rows, HG_KEY), F32)
    wide_bf16 = pltpu.VMEM((pass_rows, HG_KEY), BF16)
    perm = pltpu.VMEM((t_all, t_all), BF16)
    return pl.pallas_call(
        functools.partial(_hg_kernel, log2_rows=log2_rows, log2_ncol=log2_ncol, pass_rows=pass_rows),
        grid=(nb, 2, nsteps),
        in_specs=[h_spec,
                  pl.BlockSpec(wq.shape, lambda b, d, s: (0, 0), pipeline_mode=once),
                  pl.BlockSpec((None, d_model, HG_KEY), lambda b, d, s: (d, 0, 0), pipeline_mode=once),
                  pl.BlockSpec(wi.shape, lambda b, d, s: (0, 0), pipeline_mode=once),
                  pl.BlockSpec((None, 1, HG_KEY), lambda b, d, s: (d, 0, 0)),
                  st_spec],
        out_specs=[o_spec, st_spec],
        out_shape=[jax.ShapeDtypeStruct((2,) + h_grid.shape, F32),
                   jax.ShapeDtypeStruct(s_in.shape, F32)],
        scratch_shapes=[perm, perm, wide_f32, wide_f32, wide_f32, wide_bf16, wide_bf16, wide_bf16,
                        pltpu.VMEM((SUBLANES, HG_KEY), F32), wide_f32,
                        pltpu.VMEM((HG_HEADS, HG_DIM, HG_DIM), F32)],
        compiler_params=_cparams(3),
        name="hgrn2_scan",
    )(h_grid, wq, wf, wi, lb, s_in)


def _proj_kernel(h_ref, w_ref, o_ref, *, col_tile):
    h = h_ref[...]
    for c in range(w_ref.shape[1] // col_tile):
        sl = slice(c * col_tile, (c + 1) * col_tile)
        o_ref[:, sl] = _dot(h, w_ref[:, sl]).astype(BF16)


def _gate_proj(h_all, n_lat, w):
    nb, ltot, d = h_all.shape
    n = w.shape[1]
    ctx_tiles = (ltot - n_lat) // TOK_TILE
    return pl.pallas_call(
        functools.partial(_proj_kernel, col_tile=512),
        grid=(nb, n_lat // TOK_TILE),
        in_specs=[pl.BlockSpec((None, TOK_TILE, d), lambda b, i: (b, i + ctx_tiles, 0)),
                  pl.BlockSpec(w.shape, lambda b, i: (0, 0))],
        out_specs=pl.BlockSpec((None, TOK_TILE, n), lambda b, i: (b, i, 0)),
        out_shape=jax.ShapeDtypeStruct((nb, n_lat, n), BF16),
        compiler_params=_cparams(2),
        name="gate_proj",
    )(h_all, w)


def _merge_kernel(x_ref, yf_ref, yb_ref, of_ref, ob_ref, zg_ref, wbs_ref, wbh_ref, wo_ref,
                  gs_ref, gh_ref, gpm_ref, gpf_ref, mod_ref, wr_ref, br_ref,
                  xl_ref, h2_ref, eid_ref, ew_ref):
    b = pl.program_id(0)
    tm = x_ref.shape[0]
    d = D_MODEL
    z = zg_ref[:, 0:SSM_D_INNER].astype(F32)
    yz = (yf_ref[...].astype(F32) + yb_ref[...].astype(F32)) * _silu(z)
    parts = []
    for g in range(SSM_GROUPS):
        gsl = slice(g * SSM_GROUP_WIDTH, (g + 1) * SSM_GROUP_WIDTH)
        parts.append(_rms(yz[:, gsl], gs_ref[:, gsl]).astype(BF16))
    br_ssm = _dot(jnp.concatenate(parts, axis=1), wbs_ref[...])

    o = of_ref[...].astype(F32) + ob_ref[...].astype(F32)
    og = zg_ref[:, SSM_D_INNER:SSM_D_INNER + d].astype(F32)
    parts = []
    for hh in range(HG_HEADS):
        hsl = slice(hh * HG_DIM, (hh + 1) * HG_DIM)
        parts.append((_rms(o[:, hsl], gh_ref[...]) * _silu(og[:, hsl])).astype(BF16))
    br_hg = _dot(jnp.concatenate(parts, axis=1), wbh_ref[...])

    gm = zg_ref[:, SSM_D_INNER + d:SSM_D_INNER + 2 * d].astype(F32)
    gh = zg_ref[:, SSM_D_INNER + 2 * d:SSM_D_INNER + 3 * d].astype(F32)
    mixed = _sigmoid(gm) * br_ssm + _sigmoid(gh) * br_hg
    mix = _dot(mixed.astype(BF16), wo_ref[...])

    gt1 = mod_ref[pl.ds(b, 1), pl.ds(2 * d, d)]
    sh2 = mod_ref[pl.ds(b, 1), pl.ds(3 * d, d)]
    sc2 = mod_ref[pl.ds(b, 1), pl.ds(4 * d, d)]
    xl = x_ref[...] + gt1 * _rms(mix, gpm_ref[...])
    xl_ref[...] = xl
    h2 = _rms(xl, gpf_ref[...]) * (1.0 + sc2) + sh2
    h2_ref[...] = h2

    logits = jnp.dot(h2, wr_ref[...], precision=HIGHEST, preferred_element_type=F32) + br_ref[...]
    lane = lax.broadcasted_iota(I32, (tm, LANES), 1)
    lane_f = lane.astype(F32)
    neg = jnp.float32(-jnp.inf)
    far = jnp.float32(2 * LANES)
    gl = jnp.where(lane < MOE_GROUPS, logits, neg)
    gmax = jnp.max(gl, axis=1, keepdims=True)
    p_grp = 1.0 / jnp.sum(jnp.exp(gl - gmax), axis=1, keepdims=True)
    grp = jnp.min(jnp.where(gl == gmax, lane_f, far), axis=1, keepdims=True)
    lo = MOE_GROUPS + MOE_EXPERTS_PER_GROUP * grp
    el = jnp.where(jnp.logical_and(lane_f >= lo, lane_f < lo + MOE_EXPERTS_PER_GROUP), logits, neg)
    m1 = jnp.max(el, axis=1, keepdims=True)
    i1 = jnp.min(jnp.where(el == m1, lane_f, far), axis=1, keepdims=True)
    el2 = jnp.where(lane_f == i1, neg, el)
    m2 = jnp.max(el2, axis=1, keepdims=True)
    i2 = jnp.min(jnp.where(el2 == m2, lane_f, far), axis=1, keepdims=True)
    e2 = jnp.exp(m2 - m1)
    w1 = p_grp / (1.0 + e2)
    w2 = p_grp * e2 / (1.0 + e2)
    eid = jnp.where(lane == 0, i1, i2) - float(MOE_GROUPS)
    eid_ref[...] = jnp.where(lane < 2, eid, 0.0).astype(I32)
    ew_ref[...] = jnp.where(lane == 0, w1, jnp.where(lane == 1, w2, 0.0))


def _merge(x, y, o, zg, w_bs, w_bh, w_o, g_ssm, g_hg, g_pm, g_pf, mod, w_r, b_r):
    nb, n_lat, d = x.shape
    tm = TOK_TILE
    ctx_tiles = (y.shape[2] - n_lat) // tm

    def full(a):
        return pl.BlockSpec(a.shape, lambda b, i: (0,) * a.ndim)

    tok = lambda w: pl.BlockSpec((None, tm, w), lambda b, i: (b, i, 0))
    return pl.pallas_call(
        _merge_kernel,
        grid=(nb, n_lat // tm),
        in_specs=[tok(d),
                  pl.BlockSpec((None, None, tm, SSM_D_INNER), lambda b, i: (0, b, i + ctx_tiles, 0)),
                  pl.BlockSpec((None, None, tm, SSM_D_INNER), lambda b, i: (1, b, i + ctx_tiles, 0)),
                  pl.BlockSpec((None, None, tm, d), lambda b, i: (0, b, i, 0)),
                  pl.BlockSpec((None, None, tm, d), lambda b, i: (1, b, i, 0)),
                  tok(zg.shape[2]),
                  full(w_bs), full(w_bh), full(w_o), full(g_ssm), full(g_hg), full(g_pm), full(g_pf),
                  full(mod), full(w_r), full(b_r)],
        out_specs=[tok(d), tok(d), tok(LANES), tok(LANES)],
        out_shape=[jax.ShapeDtypeStruct((nb, n_lat, d), F32),
                   jax.ShapeDtypeStruct((nb, n_lat, d), F32),
                   jax.ShapeDtypeStruct((nb, n_lat, LANES), I32),
                   jax.ShapeDtypeStruct((nb, n_lat, LANES), F32)],
        compiler_params=_cparams(2),
        name="merge_route",
    )(x, y, y, o, o, zg, w_bs, w_bh, w_o, g_ssm, g_hg, g_pm, g_pf, mod, w_r, b_r)


def _rank_kernel(eid_ref, dest_ref, blk_ref, cnt, base):
    p = pl.program_id(0)
    i = pl.program_id(1)
    t = eid_ref.shape[0]
    lane = lax.broadcasted_iota(I32, (t, LANES), 1)
    e = eid_ref[...]
    oh0 = (lane == e[:, 0:1]).astype(F32)
    oh1 = (lane == e[:, 1:2]).astype(F32)
    c0 = jnp.sum(oh0, axis=0, keepdims=True)
    c1 = jnp.sum(oh1, axis=0, keepdims=True)

    @pl.when(jnp.logical_and(p == 0, i == 0))
    def _():
        cnt[...] = jnp.zeros(cnt.shape, F32)

    @pl.when(p == 0)
    def _():
        cnt[...] = cnt[...] + c0 + c1

    @pl.when(jnp.logical_and(p == 1, i == 0))
    def _():
        lane1 = lax.broadcasted_iota(I32, (SUBLANES, LANES), 1)
        padded = jnp.floor((cnt[...] + (MOE_ROWS - 1)) / MOE_ROWS) * MOE_ROWS
        padded = jnp.broadcast_to(jnp.where(lane1[0:1] < MOE_EXPERTS, padded, 0.0), (SUBLANES, LANES))
        rr = lax.broadcasted_iota(I32, (LANES, LANES), 0)
        cc = lax.broadcasted_iota(I32, (LANES, LANES), 1)
        start = jnp.dot(padded, (rr < cc).astype(F32), precision=HIGHEST, preferred_element_type=F32)
        base[...] = start[0:1]
        pad_end = start[0:1] + padded[0:1]
        pad_end = jnp.where(lane1[0:1] < MOE_EXPERTS, pad_end, jnp.float32(2.0 ** 30))
        nblk = blk_ref.shape[0]
        bstart = lax.broadcasted_iota(I32, (nblk, LANES), 0).astype(F32) * MOE_ROWS
        owner = jnp.sum((pad_end <= bstart).astype(F32), axis=1, keepdims=True)
        owner = jnp.minimum(owner, MOE_EXPERTS - 1)
        total = jnp.max(jnp.where(lane1[0:1] == MOE_EXPERTS - 1, pad_end, 0.0), axis=1, keepdims=True)
        lane_b = lax.broadcasted_iota(I32, (nblk, LANES), 1)
        blk_ref[...] = jnp.where(lane_b == 0, owner, total / MOE_ROWS).astype(I32)

    @pl.when(p == 1)
    def _():
        rr = lax.broadcasted_iota(I32, (t, t), 0)
        cc = lax.broadcasted_iota(I32, (t, t), 1)
        before = (rr > cc).astype(BF16)
        b0 = base[...]
        r0 = _dot(before, oh0.astype(BF16)) + b0
        r1 = _dot(before, oh1.astype(BF16)) + b0 + c0
        d0 = jnp.sum(oh0 * r0, axis=1, keepdims=True)
        d1 = jnp.sum(oh1 * r1, axis=1, keepdims=True)
        dest_ref[...] = jnp.where(lane == 0, d0, jnp.where(lane == 1, d1, 0.0)).astype(I32)
        base[...] = b0 + c0 + c1


def _rank(eid, n_blocks):
    n = eid.shape[0]
    t = RANK_TILE
    nblk_pad = -(-n_blocks // SUBLANES) * SUBLANES
    return pl.pallas_call(
        _rank_kernel,
        grid=(2, n // t),
        in_specs=[pl.BlockSpec((t, LANES), lambda p, i: (i, 0))],
        out_specs=[pl.BlockSpec((t, LANES), lambda p, i: (p * i, 0)),
                   pl.BlockSpec((nblk_pad, LANES), lambda p, i: (0, 0))],
        out_shape=[jax.ShapeDtypeStruct((n, LANES), I32),
                   jax.ShapeDtypeStruct((nblk_pad, LANES), I32)],
        scratch_shapes=[pltpu.VMEM((1, LANES), F32), pltpu.VMEM((1, LANES), F32)],
        compiler_params=_cparams(2),
        name="moe_rank",
    )(eid)


def _row_copy(src, s, dst, t, sem):
    return pltpu.make_async_copy(src.at[pl.ds(s, 1)], dst.at[pl.ds(t, 1)], sem)


def _dispatch_kernel(dest_ref, h_ref, zin_ref, out_ref, sem):
    del zin_ref
    i = pl.program_id(0)
    t = h_ref.shape[0]

    def start(r, carry):
        base = (i * t + r) * 2
        _row_copy(h_ref, r, out_ref, dest_ref[base], sem).start()
        _row_copy(h_ref, r, out_ref, dest_ref[base + 1], sem).start()
        return carry

    lax.fori_loop(0, t, start, 0)

    def wait(r, carry):
        _row_copy(h_ref, 0, out_ref, 0, sem).wait()
        _row_copy(h_ref, 0, out_ref, 0, sem).wait()
        return carry

    lax.fori_loop(0, t, wait, 0)


def _dispatch(dest_flat, h2, cap):
    n, d = h2.shape
    t = TOK_TILE
    return pl.pallas_call(
        _dispatch_kernel,
        grid_spec=pltpu.PrefetchScalarGridSpec(
            num_scalar_prefetch=1,
            grid=(n // t,),
            in_specs=[pl.BlockSpec((t, d), lambda i, dest: (i, 0)),
                      pl.BlockSpec(memory_space=pl.ANY)],
            out_specs=pl.BlockSpec(memory_space=pl.ANY),
            scratch_shapes=[pltpu.SemaphoreType.DMA]),
        out_shape=jax.ShapeDtypeStruct((cap, d), F32),
        input_output_aliases={2: 0},
        compiler_params=_cparams(1),
        name="moe_dispatch",
    )(dest_flat, h2, jnp.zeros((cap, d), F32))


def _expert_kernel(be_ref, na_ref, x_ref, wg_ref, wu_ref, wd_ref, y_ref, wg_s, wu_s, wd_s):
    i = pl.program_id(0)
    e = be_ref[i]
    prev = be_ref[jnp.maximum(i - 1, 0)]

    @pl.when(jnp.logical_or(i == 0, e != prev))
    def _():
        wg_s[...] = wg_ref[...].astype(BF16)
        wu_s[...] = wu_ref[...].astype(BF16)
        wd_s[...] = wd_ref[...].astype(BF16)

    @pl.when(i < na_ref[0])
    def _():
        x = x_ref[...].astype(BF16)
        hid = _silu(_dot(x, wg_s[...])) * _dot(x, wu_s[...])
        y_ref[...] = _dot(hid.astype(BF16), wd_s[...])

    @pl.when(i >= na_ref[0])
    def _():
        y_ref[...] = jnp.zeros(y_ref.shape, F32)


def _experts(blk_e, n_act, xs, w_gate, w_up, w_down):
    cap, d = xs.shape
    r = MOE_ROWS
    ff = w_gate.shape[2]
    return pl.pallas_call(
        _expert_kernel,
        grid_spec=pltpu.PrefetchScalarGridSpec(
            num_scalar_prefetch=2,
            grid=(cap // r,),
            in_specs=[pl.BlockSpec((r, d), lambda i, be, na: (i, 0)),
                      pl.BlockSpec((None, d, ff), lambda i, be, na: (be[i], 0, 0)),
                      pl.BlockSpec((None, d, ff), lambda i, be, na: (be[i], 0, 0)),
                      pl.BlockSpec((None, ff, d), lambda i, be, na: (be[i], 0, 0))],
            out_specs=pl.BlockSpec((r, d), lambda i, be, na: (i, 0)),
            scratch_shapes=[pltpu.VMEM((d, ff), BF16), pltpu.VMEM((d, ff), BF16),
                            pltpu.VMEM((ff, d), BF16)]),
        out_shape=jax.ShapeDtypeStruct((cap, d), F32),
        compiler_params=_cparams(1),
        name="moe_experts",
    )(blk_e, n_act, xs, w_gate, w_up, w_down)


def _combine_kernel(dest_ref, ys_ref, xl_ref, ew_ref, mod_ref, g_ref, out_ref, buf, sem, *, tiles_per_batch):
    i = pl.program_id(0)
    t = xl_ref.shape[0]

    def start(r, carry):
        base = (i * t + r) * 2
        _row_copy(ys_ref, dest_ref[base], buf.at[0], r, sem).start()
        _row_copy(ys_ref, dest_ref[base + 1], buf.at[1], r, sem).start()
        return carry

    lax.fori_loop(0, t, start, 0)

    def wait(r, carry):
        _row_copy(ys_ref, 0, buf.at[0], 0, sem).wait()
        _row_copy(ys_ref, 0, buf.at[1], 0, sem).wait()
        return carry

    lax.fori_loop(0, t, wait, 0)
    b = i // tiles_per_batch
    ew = ew_ref[...]
    ffn = buf[0] * ew[:, 0:1] + buf[1] * ew[:, 1:2]
    gt2 = mod_ref[pl.ds(b, 1), pl.ds(5 * D_MODEL, D_MODEL)]
    out_ref[...] = xl_ref[...] + gt2 * _rms(ffn, g_ref[...])


def _combine(dest_flat, ys, xl, ew, mod, g, tiles_per_batch):
    n, d = xl.shape
    t = TOK_TILE
    return pl.pallas_call(
        functools.partial(_combine_kernel, tiles_per_batch=tiles_per_batch),
        grid_spec=pltpu.PrefetchScalarGridSpec(
            num_scalar_prefetch=1,
            grid=(n // t,),
            in_specs=[pl.BlockSpec(memory_space=pl.ANY),
                      pl.BlockSpec((t, d), lambda i, dest: (i, 0)),
                      pl.BlockSpec((t, LANES), lambda i, dest: (i, 0)),
                      pl.BlockSpec(mod.shape, lambda i, dest: (0, 0)),
                      pl.BlockSpec(g.shape, lambda i, dest: (0, 0))],
            out_specs=pl.BlockSpec((t, d), lambda i, dest: (i, 0)),
            scratch_shapes=[pltpu.VMEM((2, t, d), F32), pltpu.SemaphoreType.DMA]),
        out_shape=jax.ShapeDtypeStruct((n, d), F32),
        compiler_params=_cparams(1),
        name="moe_combine",
    )(dest_flat, ys, xl, ew, mod, g)


def kernel(x, c, ctx, c_ctx, w_ada, b_ada, g_pre_mix, g_post_mix, g_pre_ffn, g_post_ffn, w_in,
           conv_w, conv_b, dt_bias, a_log, d_skip, g_ssm_norm, hg_lb, g_hg_norm, w_branch_ssm,
           w_branch_hg, w_out, w_group_router, b_group_router, w_expert_router, b_expert_router,
           w_gate, w_up, w_down):
    nb, n_lat, d = x.shape
    n_ctx = ctx.shape[1]
    rows = n_lat // GRID_W
    assert w_ada.shape[0] == 1 and d == D_MODEL and nb + 1 <= SUBLANES
    assert rows % HG_CHUNK == 0 and n_ctx % (2 * HG_CHUNK) == 0

    w = w_in[0]
    o0 = 0
    w_z = w[:, o0:o0 + SSM_D_INNER]; o0 += SSM_D_INNER
    w_xbc = w[:, o0:o0 + SSM_CONV_DIM].astype(BF16); o0 += SSM_CONV_DIM
    w_dtr = w[:, o0:o0 + 2 * SSM_HEADS]; o0 += 2 * SSM_HEADS
    w_q = w[:, o0:o0 + HG_KEY].astype(BF16); o0 += HG_KEY
    w_f = jnp.stack([w[:, o0:o0 + HG_KEY], w[:, o0 + HG_KEY:o0 + 2 * HG_KEY]]).astype(BF16); o0 += 2 * HG_KEY
    w_i = w[:, o0:o0 + HG_KEY].astype(BF16); o0 += HG_KEY
    w_zg = jnp.concatenate([w_z, w[:, o0:o0 + 3 * d]], axis=1).astype(BF16)
    pad = jnp.zeros((d, LANES - SSM_HEADS), F32)
    w_dt = jnp.concatenate([w_dtr[:, :SSM_HEADS], pad, w_dtr[:, SSM_HEADS:], pad], axis=1).astype(BF16)
    zpad = jnp.zeros((LANES - SSM_HEADS,), F32)
    dtb = jnp.concatenate([dt_bias[0, 0], zpad, dt_bias[0, 1], zpad]).reshape(1, 2 * LANES)
    a_log_p = jnp.pad(a_log[0], ((0, 0), (0, LANES - SSM_HEADS))).reshape(2, 1, LANES)
    dskip_e = jnp.repeat(d_skip[0], SSM_HEAD_DIM, axis=-1).reshape(2, 1, SSM_D_INNER)
    head_of_lane = jnp.arange(SSM_D_INNER, dtype=I32) // SSM_HEAD_DIM
    expand = (jnp.arange(LANES, dtype=I32)[:, None] == head_of_lane[None, :]).astype(BF16)
    lb = jax.nn.softmax(hg_lb.astype(F32), axis=0)[0].reshape(2, 1, HG_KEY)
    w_r = jnp.concatenate([w_group_router[0],
                           jnp.transpose(w_expert_router[0], (1, 0, 2)).reshape(d, MOE_EXPERTS),
                           jnp.zeros((d, LANES - MOE_GROUPS - MOE_EXPERTS), F32)], axis=1)
    b_r = jnp.concatenate([b_group_router[0], b_expert_router[0].reshape(-1),
                           jnp.zeros((LANES - MOE_GROUPS - MOE_EXPERTS,), F32)]).reshape(1, LANES)
    g_hg = g_hg_norm[0].reshape(1, HG_DIM)

    cc = jnp.concatenate([c, c_ctx[None], jnp.zeros((SUBLANES - nb - 1, d), F32)], axis=0)
    mod = _modulation(cc, w_ada[0], b_ada[0])
    h_all, h_lat = _pre_norm(x, ctx, mod, g_pre_mix)

    xbc, dt = _ssd_inputs(h_all, w_xbc, w_dt, conv_w[0], conv_b, dtb)
    y = _ssd_scan(xbc, dt, a_log_p, dskip_e, expand)

    s0 = jnp.zeros((nb, 2, HG_HEADS, HG_DIM, HG_DIM), F32)
    _, s_ctx = _hg_scan(h_all[:, :n_ctx].astype(F32), 1, w_q, w_f, w_i, lb, s0)
    o_grid, _ = _hg_scan(h_lat.reshape(nb, rows, GRID_W, d), HG_COLS_PER_STEP, w_q, w_f, w_i, lb, s_ctx)
    o = o_grid.reshape(2, nb, n_lat, d)

    zg = _gate_proj(h_all, n_lat, w_zg)
    xl, h2, eid, ew = _merge(x, y, o, zg, w_branch_ssm[0].astype(BF16), w_branch_hg[0].astype(BF16),
                             w_out[0].astype(BF16), g_ssm_norm, g_hg, g_post_mix, g_pre_ffn, mod,
                             w_r, b_r)

    n = nb * n_lat
    cap = -(-(2 * n + MOE_EXPERTS * MOE_ROWS) // MOE_ROWS) * MOE_ROWS
    n_blocks = cap // MOE_ROWS
    dest, blk = _rank(eid.reshape(n, LANES), n_blocks)
    dest_flat = dest[:, 0:2].reshape(-1)
    xs = _dispatch(dest_flat, h2.reshape(n, d), cap)
    ys = _experts(blk[:n_blocks, 0], blk[0:1, 1], xs, w_gate[0], w_up[0], w_down[0])
    out = _combine(dest_flat, ys, xl.reshape(n, d), ew.reshape(n, LANES), mod, g_post_ffn,
                   n_lat // TOK_TILE)
    return out.reshape(nb, n_lat, d)
```

```python
import functools

import jax
import jax.numpy as jnp
from jax import lax
from jax.experimental import pallas as pl
from jax.experimental.pallas import tpu as pltpu

F32 = jnp.float32
BF16 = jnp.bfloat16
I32 = jnp.int32
HIGHEST = lax.Precision.HIGHEST

D_MODEL = 1024
GRID_W = 64
RMS_EPS = 1e-6
SSM_D_INNER = 2048
SSM_HEAD_DIM = 64
SSM_HEADS = 32
SSM_GROUPS = 4
SSM_HEADS_PER_GROUP = 8
SSM_STATE = 128
SSM_GROUP_WIDTH = SSM_HEADS_PER_GROUP * SSM_HEAD_DIM
SSM_CONV_DIM = SSM_D_INNER + 2 * SSM_GROUPS * SSM_STATE
HG_HEADS = 8
HG_DIM = 128
HG_KEY = HG_HEADS * HG_DIM
MOE_GROUPS = 4
MOE_EXPERTS_PER_GROUP = 8
MOE_EXPERTS = 32
MOE_D_FF = 512

LANES = 128
SUBLANES = 8
VMEM_LIMIT_BYTES = 56 * 1024 * 1024

TOK_TILE = 256
SSD_CHUNK = 128
HG_CHUNK = 64
HG_SUB = 16
HG_COLS_PER_STEP = 8
HG_COLS_PER_PASS = 4
MOE_ROWS = 256
RANK_TILE = 512
DMA_UNROLL = 8


def _cparams(n_axes):
    return pltpu.CompilerParams(dimension_semantics=("arbitrary",) * n_axes,
                                vmem_limit_bytes=VMEM_LIMIT_BYTES)


def _sigmoid(x):
    return 0.5 * jnp.tanh(0.5 * x) + 0.5


def _silu(x):
    return x * _sigmoid(x)


def _rms(v, g):
    ms = jnp.mean(v * v, axis=-1, keepdims=True)
    return v * lax.rsqrt(ms + RMS_EPS) * g


def _dot(a, b):
    return jnp.dot(a, b, preferred_element_type=F32)


def _dot_nt(a, b):
    return lax.dot_general(a, b, (((1,), (1,)), ((), ())), preferred_element_type=F32)


def _dot_tn(a, b):
    return lax.dot_general(a, b, (((0,), (0,)), ((), ())), preferred_element_type=F32)


def _mod_kernel(c_ref, w_ref, b_ref, o_ref):
    s = _silu(c_ref[...])
    o_ref[...] = jnp.dot(s, w_ref[...], precision=HIGHEST, preferred_element_type=F32) + b_ref[...]


def _modulation(cc, w_ada, b_ada):
    n = w_ada.shape[1]
    tn = 1024
    return pl.pallas_call(
        _mod_kernel,
        grid=(n // tn,),
        in_specs=[pl.BlockSpec((SUBLANES, D_MODEL), lambda j: (0, 0)),
                  pl.BlockSpec((D_MODEL, tn), lambda j: (0, j)),
                  pl.BlockSpec((1, tn), lambda j: (0, j))],
        out_specs=pl.BlockSpec((SUBLANES, tn), lambda j: (0, j)),
        out_shape=jax.ShapeDtypeStruct((SUBLANES, n), F32),
        compiler_params=_cparams(1),
        name="adaln_mod",
    )(cc, w_ada, b_ada.reshape(1, n))


def _norm_kernel(x_ref, ctx_ref, mod_ref, g_ref, all_ref, lat_ref, *, ctx_row):
    b = pl.program_id(0)
    i = pl.program_id(1)
    g = g_ref[...]

    def nm(v, row):
        sh = mod_ref[pl.ds(row, 1), pl.ds(0, D_MODEL)]
        sc = mod_ref[pl.ds(row, 1), pl.ds(D_MODEL, D_MODEL)]
        return _rms(v, g) * (1.0 + sc) + sh

    @pl.when(i == 0)
    def _():
        all_ref[...] = nm(ctx_ref[...], ctx_row).astype(BF16)

    @pl.when(i > 0)
    def _():
        h = nm(x_ref[...], b)
        all_ref[...] = h.astype(BF16)
        lat_ref[...] = h


def _pre_norm(x, ctx, mod, g):
    nb, n_lat, d = x.shape
    n_ctx = ctx.shape[1]
    assert n_ctx == TOK_TILE and n_lat % TOK_TILE == 0
    nt = n_lat // TOK_TILE
    return pl.pallas_call(
        functools.partial(_norm_kernel, ctx_row=nb),
        grid=(nb, nt + 1),
        in_specs=[pl.BlockSpec((None, TOK_TILE, d), lambda b, i: (b, jnp.maximum(i - 1, 0), 0)),
                  pl.BlockSpec((None, TOK_TILE, d), lambda b, i: (b, 0, 0)),
                  pl.BlockSpec(mod.shape, lambda b, i: (0, 0)),
                  pl.BlockSpec((1, d), lambda b, i: (0, 0))],
        out_specs=[pl.BlockSpec((None, TOK_TILE, d), lambda b, i: (b, i, 0)),
                   pl.BlockSpec((None, TOK_TILE, d), lambda b, i: (b, jnp.maximum(i - 1, 0), 0))],
        out_shape=[jax.ShapeDtypeStruct((nb, n_ctx + n_lat, d), BF16),
                   jax.ShapeDtypeStruct((nb, n_lat, d), F32)],
        compiler_params=_cparams(2),
        name="pre_norm",
    )(x, ctx, mod, g)


XBC_COL_TILE = 512


def _xbc_kernel(h_ref, hp_ref, hn_ref, w_ref, wdt_ref, cw_ref, cb_ref, dtb_ref, xbc_ref, dt_ref,
                raw_s, *, n_tiles):
    i = pl.program_id(1)
    h = h_ref[...]
    tm = h.shape[0]
    halo = hp_ref.shape[0]
    prev_ok = (i >= 2).astype(BF16)
    next_ok = jnp.logical_and(i >= 1, i < n_tiles - 1).astype(BF16)
    rows = jnp.concatenate([hp_ref[...] * prev_ok, h, hn_ref[...] * next_ok], axis=0)
    for c in range(SSM_CONV_DIM // XBC_COL_TILE):
        sl = slice(c * XBC_COL_TILE, (c + 1) * XBC_COL_TILE)
        raw_s[...] = _dot(rows, w_ref[:, sl])
        up = raw_s[halo - 1:halo - 1 + tm, :]
        mid = raw_s[halo:halo + tm, :]
        dn = raw_s[halo + 1:halo + 1 + tm, :]
        conv = cw_ref[0:1, sl] * up + cw_ref[1:2, sl] * mid + cw_ref[2:3, sl] * dn + cb_ref[:, sl]
        xbc_ref[:, sl] = _silu(conv).astype(BF16)
    u = _dot(h, wdt_ref[...]) + dtb_ref[...]
    dt_ref[...] = jnp.maximum(u, 0.0) + jnp.log(1.0 + jnp.exp(-jnp.abs(u)))


def _ssd_inputs(h_all, w_xbc, w_dt, conv_w, conv_b, dt_bias):
    nb, ltot, d = h_all.shape
    nt = ltot // TOK_TILE
    halo = 2 * SUBLANES
    n_halo = ltot // halo
    per = TOK_TILE // halo
    return pl.pallas_call(
        functools.partial(_xbc_kernel, n_tiles=nt),
        grid=(nb, nt),
        in_specs=[pl.BlockSpec((None, TOK_TILE, d), lambda b, i: (b, i, 0)),
                  pl.BlockSpec((None, halo, d), lambda b, i: (b, jnp.maximum(i * per - 1, 0), 0)),
                  pl.BlockSpec((None, halo, d), lambda b, i: (b, jnp.minimum((i + 1) * per, n_halo - 1), 0)),
                  pl.BlockSpec(w_xbc.shape, lambda b, i: (0, 0)),
                  pl.BlockSpec(w_dt.shape, lambda b, i: (0, 0)),
                  pl.BlockSpec(conv_w.shape, lambda b, i: (0, 0)),
                  pl.BlockSpec(conv_b.shape, lambda b, i: (0, 0)),
                  pl.BlockSpec(dt_bias.shape, lambda b, i: (0, 0))],
        out_specs=[pl.BlockSpec((None, TOK_TILE, SSM_CONV_DIM), lambda b, i: (b, i, 0)),
                   pl.BlockSpec((None, TOK_TILE, 2 * LANES), lambda b, i: (b, i, 0))],
        out_shape=[jax.ShapeDtypeStruct((nb, ltot, SSM_CONV_DIM), BF16),
                   jax.ShapeDtypeStruct((nb, ltot, 2 * LANES), F32)],
        scratch_shapes=[pltpu.VMEM((TOK_TILE + 2 * halo, XBC_COL_TILE), F32)],
        compiler_params=_cparams(2),
        name="ssd_inputs",
    )(h_all, h_all, h_all, w_xbc, w_dt, conv_w, conv_b, dt_bias)


def _ssd_kernel(xbc_ref, dt_ref, alog_ref, dskip_ref, e_ref, y_ref, state):
    d = pl.program_id(1)
    step = pl.program_id(2)
    q = SSD_CHUNK

    @pl.when(step == 0)
    def _():
        state[...] = jnp.zeros(state.shape, F32)

    lane = lax.broadcasted_iota(I32, (1, LANES), 1)
    a = jnp.where(lane < SSM_HEADS, -jnp.exp(alog_ref[...]), 0.0)
    dt = dt_ref[...]
    adt = dt * a
    r = lax.broadcasted_iota(I32, (q, q), 0)
    c = lax.broadcasted_iota(I32, (q, q), 1)
    mask = jnp.where(d == 0, r - c, c - r) >= 0
    cs = jnp.dot(mask.astype(F32), adt, precision=HIGHEST, preferred_element_type=F32)
    tot = jnp.sum(adt, axis=0, keepdims=True)
    cs_t = cs.T
    lhs = jnp.concatenate([dt, jnp.exp(cs), jnp.exp(tot - cs),
                           jnp.broadcast_to(jnp.exp(tot), (SUBLANES, LANES))], axis=0).astype(BF16)
    lane_h = lax.broadcasted_iota(I32, (q, LANES), 1)
    for g in range(SSM_GROUPS):
        gs = slice(g * SSM_GROUP_WIDTH, (g + 1) * SSM_GROUP_WIDTH)
        ex = _dot(lhs, e_ref[:, gs])
        dt_e, e1_e, e2_e, etot_e = ex[0:q], ex[q:2 * q], ex[2 * q:3 * q], ex[3 * q:3 * q + 1]
        x = xbc_ref[:, gs].astype(F32)
        bg = xbc_ref[:, SSM_D_INNER + g * SSM_STATE:SSM_D_INNER + (g + 1) * SSM_STATE]
        cg = xbc_ref[:, SSM_D_INNER + (SSM_GROUPS + g) * SSM_STATE:
                     SSM_D_INNER + (SSM_GROUPS + g + 1) * SSM_STATE]
        xdt = x * dt_e
        cb = _dot_nt(cg, bg)
        s_in = state[g]
        y_off = _dot(cg, s_in.astype(BF16)) * e1_e
        state[g] = s_in * etot_e + _dot_tn(bg, (xdt * e2_e).astype(BF16))
        parts = []
        for m in range(SSM_HEADS_PER_GROUP // 2):
            xp = xdt[:, m * LANES:(m + 1) * LANES]
            acc = None
            for half in range(2):
                hd = g * SSM_HEADS_PER_GROUP + 2 * m + half
                col = cs[:, hd:hd + 1]
                rowv = cs_t[hd:hd + 1, :]
                decay = jnp.exp(jnp.minimum(col - rowv, 0.0))
                w = jnp.where(mask, cb * decay, 0.0).astype(BF16)
                sel = (lane_h < SSM_HEAD_DIM) if half == 0 else (lane_h >= SSM_HEAD_DIM)
                t = _dot(w, jnp.where(sel, xp, 0.0).astype(BF16))
                acc = t if acc is None else acc + t
            parts.append(acc)
        y = jnp.concatenate(parts, axis=1) + y_off + x * dskip_ref[:, gs]
        y_ref[:, gs] = y.astype(BF16)


def _ssd_scan(xbc, dt, a_log_p, dskip_e, expand):
    nb, ltot, _ = xbc.shape
    q = SSD_CHUNK
    nc = ltot // q
    n_ctx_chunks = TOK_TILE // q

    def chunk(d, s):
        rev = jnp.where(s < n_ctx_chunks, n_ctx_chunks - 1 - s, nc + n_ctx_chunks - 1 - s)
        return jnp.where(d == 0, s, rev)

    return pl.pallas_call(
        _ssd_kernel,
        grid=(nb, 2, nc),
        in_specs=[pl.BlockSpec((None, q, SSM_CONV_DIM), lambda b, d, s: (b, chunk(d, s), 0)),
                  pl.BlockSpec((None, q, LANES), lambda b, d, s: (b, chunk(d, s), d)),
                  pl.BlockSpec((None, 1, LANES), lambda b, d, s: (d, 0, 0)),
                  pl.BlockSpec((None, 1, SSM_D_INNER), lambda b, d, s: (d, 0, 0)),
                  pl.BlockSpec(expand.shape, lambda b, d, s: (0, 0))],
        out_specs=pl.BlockSpec((None, None, q, SSM_D_INNER), lambda b, d, s: (d, b, chunk(d, s), 0)),
        out_shape=jax.ShapeDtypeStruct((2, nb, ltot, SSM_D_INNER), BF16),
        scratch_shapes=[pltpu.VMEM((SSM_GROUPS, SSM_STATE, SSM_GROUP_WIDTH), F32)],
        compiler_params=_cparams(3),
        name="ssd_scan",
    )(xbc, dt, a_log_p, dskip_e, expand)


def _hg_intra(q_s, k_s, b_s, r0, off):
    c = HG_CHUNK
    cols = pl.ds(off, HG_DIM)
    lane8 = lax.broadcasted_iota(I32, (SUBLANES, c), 1)
    lane_sub = lax.broadcasted_iota(I32, (HG_SUB, c), 1)
    rows = []
    for i in range(c // HG_SUB):
        lo = r0 + i * HG_SUB
        bi = b_s[lo:lo + HG_SUB, cols]
        qi = q_s[lo:lo + HG_SUB, cols]
        top = jnp.zeros((SUBLANES, c), F32)
        bot = jnp.zeros((SUBLANES, c), F32)
        for j in range(HG_SUB):
            s = lo + j
            at = i * HG_SUB + j
            brow = b_s[s:s + 1, cols]
            krow = k_s[s:s + 1, cols]
            if j < SUBLANES:
                p = qi * krow * jnp.exp2(bi - brow)
                top = jnp.where(lane8 == at, jnp.sum(p[0:SUBLANES], axis=1, keepdims=True), top)
                bot = jnp.where(lane8 == at, jnp.sum(p[SUBLANES:], axis=1, keepdims=True), bot)
            else:
                p = qi[SUBLANES:] * krow * jnp.exp2(bi[SUBLANES:] - brow)
                bot = jnp.where(lane8 == at, jnp.sum(p, axis=1, keepdims=True), bot)
        a_row = jnp.concatenate([top, bot], axis=0)
        if i > 0:
            bref = b_s[lo - 1:lo, cols]
            qt = qi * jnp.exp2(bi - bref)
            kt = k_s[r0:lo, cols] * jnp.exp2(bref - b_s[r0:lo, cols])
            ktp = jnp.concatenate([kt, jnp.zeros((c - i * HG_SUB, HG_DIM), F32)], axis=0)
            a_off = _dot_nt(qt.astype(BF16), ktp.astype(BF16))
            a_row = jnp.where(lane_sub < i * HG_SUB, a_off, a_row)
        rows.append(a_row)
    a = jnp.concatenate(rows, axis=0)
    r64 = lax.broadcasted_iota(I32, (c, c), 0)
    c64 = lax.broadcasted_iota(I32, (c, c), 1)
    return jnp.where(r64 >= c64, a, 0.0)


def _hg_kernel(h_ref, wq_ref, wf_ref, wi_ref, lb_ref, sin_ref, o_ref, sout_ref,
               pin_s, pout_s, q_s, k_s, b_s, v_s, qe_s, kd_s, dec_s, o_s, state, *, log2_rows, log2_ncol,
               pass_rows):
    d = pl.program_id(1)
    step = pl.program_id(2)
    nsteps = pl.num_programs(2)
    t_all = 1 << (log2_rows + log2_ncol)
    c = HG_CHUNK
    nch = pass_rows // c

    def source_row(p):
        src = ((p & ((1 << log2_rows) - 1)) << log2_ncol) + (p >> log2_rows)
        return jnp.where(d == 0, src, t_all - 1 - src)

    @pl.when(step == 0)
    def _():
        state[...] = sin_ref[...]
        shape = (t_all, t_all)
        pin_s[...] = (lax.broadcasted_iota(I32, shape, 1)
                      == source_row(lax.broadcasted_iota(I32, shape, 0))).astype(BF16)
        pout_s[...] = (lax.broadcasted_iota(I32, shape, 0)
                       == source_row(lax.broadcasted_iota(I32, shape, 1))).astype(BF16)

    r64 = lax.broadcasted_iota(I32, (c, c), 0)
    c64 = lax.broadcasted_iota(I32, (c, c), 1)
    tril = (r64 >= c64).astype(BF16)
    h_rows =h_ref[...].reshape(t_all, HG_KEY).astype(BF16)

    for ps in range(t_all // pass_rows):
        prs = slice(ps * pass_rows, (ps + 1) * pass_rows)
        hb = _dot(pin_s[prs, :], h_rows).astype(BF16)
        q_s[...] = _silu(_dot(hb, wq_ref[...]))
        fz = _dot(hb, wf_ref[...])
        t = jnp.exp(-jnp.abs(fz))
        r = 1.0 / (1.0 + t)
        big, small = r, t * r
        pos = fz >= 0.0
        lb = lb_ref[...]
        lf2 = jnp.log2(lb + (1.0 - lb) * jnp.where(pos, big, small))
        k_s[...] = (1.0 - lb) * jnp.where(pos, small, big)
        v_s[...] = _dot(hb, wi_ref[...]).astype(BF16)
        for ch in range(nch):
            rs = slice(ch * c, (ch + 1) * c)
            x = lf2[rs]
            hi = x.astype(BF16)
            rem = x - hi.astype(F32)
            mid = rem.astype(BF16)
            low = (rem - mid.astype(F32)).astype(BF16)
            b = _dot(tril, hi) + _dot(tril, mid) + _dot(tril, low)
            b_s[rs, :] = b
            btot = b[c - 1:c]
            qe_s[rs, :] = (q_s[rs, :] * jnp.exp2(b)).astype(BF16)
            kd_s[rs, :] = (k_s[rs, :] * jnp.exp2(btot - b)).astype(BF16)
            dec_s[ch:ch + 1, :] = jnp.exp2(btot)

        def head(hh, carry):
            off = pl.multiple_of(hh * HG_DIM, HG_DIM)
            cols = pl.ds(off, HG_DIM)
            s_t = state[hh]
            for ch in range(nch):
                rs = slice(ch * c, (ch + 1) * c)
                a = _hg_intra(q_s, k_s, b_s, ch * c, off)
                v = v_s[rs, cols]
                o_s[rs, cols] = _dot_nt(qe_s[rs, cols], s_t.astype(BF16)) + _dot(a.astype(BF16), v)
                s_t = s_t * dec_s[ch:ch + 1, cols] + _dot_tn(v, kd_s[rs, cols])
            state[hh] = s_t
            return carry

        lax.fori_loop(0, HG_HEADS, head, 0)
        back = _dot(pout_s[:, prs], o_s[...].astype(BF16)).reshape(o_ref.shape)
        if ps == 0:
            o_ref[...] = back
        else:
            o_ref[...] += back

    @pl.when(step == nsteps - 1)
    def _():
        sout_ref[...] = state[...]


def _hg_scan(h_grid, ncol, wq, wf, wi, lb, s_in):
    if h_grid.ndim == 3:
        nb, rows, d_model = h_grid.shape
        ncol_total = 1
    else:
        nb, rows, ncol_total, d_model = h_grid.shape
    nsteps = ncol_total // ncol
    t_all = rows * ncol
    pass_rows = min(t_all, HG_COLS_PER_PASS * LANES)
    log2_rows, log2_ncol = rows.bit_length() - 1, ncol.bit_length() - 1
    assert rows == 1 << log2_rows and ncol == 1 << log2_ncol and t_all % pass_rows == 0
    assert pass_rows // HG_CHUNK <= SUBLANES

    def pos(d, s):
        return jnp.where(d == 0, s, nsteps - 1 - s)

    once = pl.Buffered(1)
    if h_grid.ndim == 3:
        h_spec = pl.BlockSpec((None, rows, d_model), lambda b, d, s: (b, 0, 0))
        o_spec = pl.BlockSpec((None, None, rows, d_model), lambda b, d, s: (d, b, 0, 0))
    else:
        h_spec = pl.BlockSpec((None, rows, ncol, d_model), lambda b, d, s: (b, 0, pos(d, s), 0))
        o_spec = pl.BlockSpec((None, None, rows, ncol, d_model), lambda b, d, s: (d, b, 0, pos(d, s), 0))
    st_spec = pl.BlockSpec((None, None, HG_HEADS, HG_DIM, HG_DIM), lambda b, d, s: (b, d, 0, 0, 0))
    wide_f32 = pltpu.VMEM((pass_rows, HG_KEY), F32)
    wide_bf16 = pltpu.VMEM((pass_rows, HG_KEY), BF16)
    perm = pltpu.VMEM((t_all, t_all), BF16)
    return pl.pallas_call(
        functools.partial(_hg_kernel, log2_rows=log2_rows, log2_ncol=log2_ncol, pass_rows=pass_rows),
        grid=(nb, 2, nsteps),
        in_specs=[h_spec,
                  pl.BlockSpec(wq.shape, lambda b, d, s: (0, 0), pipeline_mode=once),
                  pl.BlockSpec((None, d_model, HG_KEY), lambda b, d, s: (d, 0, 0), pipeline_mode=once),
                  pl.BlockSpec(wi.shape, lambda b, d, s: (0, 0), pipeline_mode=once),
                  pl.BlockSpec((None, 1, HG_KEY), lambda b, d, s: (d, 0, 0)),
                  st_spec],
        out_specs=[o_spec, st_spec],
        out_shape=[jax.ShapeDtypeStruct((2,) + h_grid.shape, F32),
                   jax.ShapeDtypeStruct(s_in.shape, F32)],
        scratch_shapes=[perm, perm, wide_f32, wide_f32, wide_f32, wide_bf16, wide_bf16, wide_bf16,
                        pltpu.VMEM((SUBLANES, HG_KEY), F32), wide_f32,
                        pltpu.VMEM((HG_HEADS, HG_DIM, HG_DIM), F32)],
        compiler_params=_cparams(3),
        name="hgrn2_scan",
    )(h_grid, wq, wf, wi, lb, s_in)


def _merge_kernel(x_ref, yf_ref, yb_ref, of_ref, ob_ref, h_ref, wzg_ref, wbs_ref, wbh_ref, wo_ref,
                  gs_ref, gh_ref, gpm_ref, gpf_ref, mod_ref, wr_ref, br_ref,
                  xl_ref, h2_ref, eid_ref, ew_ref):
    b = pl.program_id(0)
    tm = x_ref.shape[0]
    d = D_MODEL
    h = h_ref[...]
    z = _dot(h, wzg_ref[:, 0:SSM_D_INNER])
    yz = (yf_ref[...].astype(F32) + yb_ref[...].astype(F32)) * _silu(z)
    parts = []
    for g in range(SSM_GROUPS):
        gsl = slice(g * SSM_GROUP_WIDTH, (g + 1) * SSM_GROUP_WIDTH)
        parts.append(_rms(yz[:, gsl], gs_ref[:, gsl]).astype(BF16))
    br_ssm = _dot(jnp.concatenate(parts, axis=1), wbs_ref[...])

    o = of_ref[...].astype(F32) + ob_ref[...].astype(F32)
    og = _dot(h, wzg_ref[:, SSM_D_INNER:SSM_D_INNER + d])
    parts = []
    for hh in range(HG_HEADS):
        hsl = slice(hh * HG_DIM, (hh + 1) * HG_DIM)
        parts.append((_rms(o[:, hsl], gh_ref[...]) * _silu(og[:, hsl])).astype(BF16))
    br_hg = _dot(jnp.concatenate(parts, axis=1), wbh_ref[...])

    gm = _dot(h, wzg_ref[:, SSM_D_INNER + d:SSM_D_INNER + 2 * d])
    gh = _dot(h, wzg_ref[:, SSM_D_INNER + 2 * d:SSM_D_INNER + 3 * d])
    mixed = _sigmoid(gm) * br_ssm + _sigmoid(gh) * br_hg
    mix = _dot(mixed.astype(BF16), wo_ref[...])

    gt1 = mod_ref[pl.ds(b, 1), pl.ds(2 * d, d)]
    sh2 = mod_ref[pl.ds(b, 1), pl.ds(3 * d, d)]
    sc2 = mod_ref[pl.ds(b, 1), pl.ds(4 * d, d)]
    xl = x_ref[...] + gt1 * _rms(mix, gpm_ref[...])
    xl_ref[...] = xl
    h2 = _rms(xl, gpf_ref[...]) * (1.0 + sc2) + sh2
    h2_ref[...] = h2

    logits = jnp.dot(h2, wr_ref[...], precision=HIGHEST, preferred_element_type=F32) + br_ref[...]
    lane = lax.broadcasted_iota(I32, (tm, LANES), 1)
    lane_f = lane.astype(F32)
    neg = jnp.float32(-jnp.inf)
    far = jnp.float32(2 * LANES)
    gl = jnp.where(lane < MOE_GROUPS, logits, neg)
    gmax = jnp.max(gl, axis=1, keepdims=True)
    p_grp = 1.0 / jnp.sum(jnp.exp(gl - gmax), axis=1, keepdims=True)
    grp = jnp.min(jnp.where(gl == gmax, lane_f, far), axis=1, keepdims=True)
    lo = MOE_GROUPS + MOE_EXPERTS_PER_GROUP * grp
    el = jnp.where(jnp.logical_and(lane_f >= lo, lane_f < lo + MOE_EXPERTS_PER_GROUP), logits, neg)
    m1 = jnp.max(el, axis=1, keepdims=True)
    i1 = jnp.min(jnp.where(el == m1, lane_f, far), axis=1, keepdims=True)
    el2 = jnp.where(lane_f == i1, neg, el)
    m2 = jnp.max(el2, axis=1, keepdims=True)
    i2 = jnp.min(jnp.where(el2 == m2, lane_f, far), axis=1, keepdims=True)
    e2 = jnp.exp(m2 - m1)
    w1 = p_grp / (1.0 + e2)
    w2 = p_grp * e2 / (1.0 + e2)
    eid = jnp.where(lane == 0, i1, i2) - float(MOE_GROUPS)
    eid_ref[...] = jnp.where(lane < 2, eid, 0.0).astype(I32)
    ew_ref[...] = jnp.where(lane == 0, w1, jnp.where(lane == 1, w2, 0.0))


def _merge(x, y, o, h_all, w_zg, w_bs, w_bh, w_o, g_ssm, g_hg, g_pm, g_pf, mod, w_r, b_r):
    nb, n_lat, d = x.shape
    tm = TOK_TILE
    ctx_tiles = (y.shape[2] - n_lat) // tm

    def full(a):
        return pl.BlockSpec(a.shape, lambda b, i: (0,) * a.ndim, pipeline_mode=pl.Buffered(1))

    tok = lambda w: pl.BlockSpec((None, tm, w), lambda b, i: (b, i, 0))
    return pl.pallas_call(
        _merge_kernel,
        grid=(nb, n_lat // tm),
        in_specs=[tok(d),
                  pl.BlockSpec((None, None, tm, SSM_D_INNER), lambda b, i: (0, b, i + ctx_tiles, 0)),
                  pl.BlockSpec((None, None, tm, SSM_D_INNER), lambda b, i: (1, b, i + ctx_tiles, 0)),
                  pl.BlockSpec((None, None, tm, d), lambda b, i: (0, b, i, 0)),
                  pl.BlockSpec((None, None, tm, d), lambda b, i: (1, b, i, 0)),
                  pl.BlockSpec((None, tm, d), lambda b, i: (b, i + ctx_tiles, 0)),
                  full(w_zg), full(w_bs), full(w_bh), full(w_o), full(g_ssm), full(g_hg), full(g_pm),
                  full(g_pf), full(mod), full(w_r), full(b_r)],
        out_specs=[tok(d), tok(d), tok(LANES), tok(LANES)],
        out_shape=[jax.ShapeDtypeStruct((nb, n_lat, d), F32),
                   jax.ShapeDtypeStruct((nb, n_lat, d), F32),
                   jax.ShapeDtypeStruct((nb, n_lat, LANES), I32),
                   jax.ShapeDtypeStruct((nb, n_lat, LANES), F32)],
        compiler_params=_cparams(2),
        name="merge_route",
    )(x, y, y, o, o, h_all, w_zg, w_bs, w_bh, w_o, g_ssm, g_hg, g_pm, g_pf, mod, w_r, b_r)


def _rank_kernel(eid_ref, dest_ref, blk_ref, cnt, base):
    p = pl.program_id(0)
    i = pl.program_id(1)
    t = eid_ref.shape[0]
    lane = lax.broadcasted_iota(I32, (t, LANES), 1)
    e = eid_ref[...]
    oh0 = (lane == e[:, 0:1]).astype(F32)
    oh1 = (lane == e[:, 1:2]).astype(F32)
    c0 = jnp.sum(oh0, axis=0, keepdims=True)
    c1 = jnp.sum(oh1, axis=0, keepdims=True)

    @pl.when(jnp.logical_and(p == 0, i == 0))
    def _():
        cnt[...] = jnp.zeros(cnt.shape, F32)

    @pl.when(p == 0)
    def _():
        cnt[...] = cnt[...] + c0 + c1

    @pl.when(jnp.logical_and(p == 1, i == 0))
    def _():
        lane1 = lax.broadcasted_iota(I32, (SUBLANES, LANES), 1)
        padded = jnp.floor((cnt[...] + (MOE_ROWS - 1)) / MOE_ROWS) * MOE_ROWS
        padded = jnp.broadcast_to(jnp.where(lane1[0:1] < MOE_EXPERTS, padded, 0.0), (SUBLANES, LANES))
        rr = lax.broadcasted_iota(I32, (LANES, LANES), 0)
        cc = lax.broadcasted_iota(I32, (LANES, LANES), 1)
        start = jnp.dot(padded, (rr < cc).astype(F32), precision=HIGHEST, preferred_element_type=F32)
        base[...] = start[0:1]
        pad_end = start[0:1] + padded[0:1]
        pad_end = jnp.where(lane1[0:1] < MOE_EXPERTS, pad_end, jnp.float32(2.0 ** 30))
        nblk = blk_ref.shape[0]
        bstart = lax.broadcasted_iota(I32, (nblk, LANES), 0).astype(F32) * MOE_ROWS
        owner = jnp.sum((pad_end <= bstart).astype(F32), axis=1, keepdims=True)
        owner = jnp.minimum(owner, MOE_EXPERTS - 1)
        total = jnp.max(jnp.where(lane1[0:1] == MOE_EXPERTS - 1, pad_end, 0.0), axis=1, keepdims=True)
        lane_b = lax.broadcasted_iota(I32, (nblk, LANES), 1)
        blk_ref[...] = jnp.where(lane_b == 0, owner, total / MOE_ROWS).astype(I32)

    @pl.when(p == 1)
    def _():
        rr = lax.broadcasted_iota(I32, (t, t), 0)
        cc = lax.broadcasted_iota(I32, (t, t), 1)
        before = (rr > cc).astype(BF16)
        b0 = base[...]
        r0 = _dot(before, oh0.astype(BF16)) + b0
        r1 = _dot(before, oh1.astype(BF16)) + b0 + c0
        d0 = jnp.sum(oh0 * r0, axis=1, keepdims=True)
        d1 = jnp.sum(oh1 * r1, axis=1, keepdims=True)
        dest_ref[...] = jnp.where(lane == 0, d0, jnp.where(lane == 1, d1, 0.0)).astype(I32)
        base[...] = b0 + c0 + c1


def _rank(eid, n_blocks):
    n = eid.shape[0]
    t = RANK_TILE
    nblk_pad = -(-n_blocks // SUBLANES) * SUBLANES
    return pl.pallas_call(
        _rank_kernel,
        grid=(2, n // t),
        in_specs=[pl.BlockSpec((t, LANES), lambda p, i: (i, 0))],
        out_specs=[pl.BlockSpec((t, LANES), lambda p, i: (p * i, 0)),
                   pl.BlockSpec((nblk_pad, LANES), lambda p, i: (0, 0))],
        out_shape=[jax.ShapeDtypeStruct((n, LANES), I32),
                   jax.ShapeDtypeStruct((nblk_pad, LANES), I32)],
        scratch_shapes=[pltpu.VMEM((1, LANES), F32), pltpu.VMEM((1, LANES), F32)],
        compiler_params=_cparams(2),
        name="moe_rank",
    )(eid)


def _row_copy(src, s, dst, t, sem):
    return pltpu.make_async_copy(src.at[pl.ds(s, 1)], dst.at[pl.ds(t, 1)], sem)


def _dispatch_kernel(dest_ref, h_ref, zin_ref, out_ref, sem):
    del zin_ref
    i = pl.program_id(0)
    t = h_ref.shape[0]

    def start(r, carry):
        base = (i * t + r) * 2
        _row_copy(h_ref, r, out_ref, dest_ref[base], sem).start()
        _row_copy(h_ref, r, out_ref, dest_ref[base + 1], sem).start()
        return carry

    lax.fori_loop(0, t, start, 0, unroll=DMA_UNROLL)

    def wait(r, carry):
        _row_copy(h_ref, 0, out_ref, 0, sem).wait()
        _row_copy(h_ref, 0, out_ref, 0, sem).wait()
        return carry

    lax.fori_loop(0, t, wait, 0, unroll=DMA_UNROLL)


def _dispatch(dest_flat, h2, cap):
    n, d = h2.shape
    t = TOK_TILE
    return pl.pallas_call(
        _dispatch_kernel,
        grid_spec=pltpu.PrefetchScalarGridSpec(
            num_scalar_prefetch=1,
            grid=(n // t,),
            in_specs=[pl.BlockSpec((t, d), lambda i, dest: (i, 0)),
                      pl.BlockSpec(memory_space=pl.ANY)],
            out_specs=pl.BlockSpec(memory_space=pl.ANY),
            scratch_shapes=[pltpu.SemaphoreType.DMA]),
        out_shape=jax.ShapeDtypeStruct((cap, d), F32),
        input_output_aliases={2: 0},
        compiler_params=_cparams(1),
        name="moe_dispatch",
    )(dest_flat, h2, jnp.zeros((cap, d), F32))


def _expert_kernel(be_ref, na_ref, x_ref, wg_ref, wu_ref, wd_ref, y_ref, wg_s, wu_s, wd_s):
    i = pl.program_id(0)
    e = be_ref[i]
    prev = be_ref[jnp.maximum(i - 1, 0)]

    @pl.when(jnp.logical_or(i == 0, e != prev))
    def _():
        wg_s[...] = wg_ref[...].astype(BF16)
        wu_s[...] = wu_ref[...].astype(BF16)
        wd_s[...] = wd_ref[...].astype(BF16)

    @pl.when(i < na_ref[0])
    def _():
        x = x_ref[...].astype(BF16)
        hid = _silu(_dot(x, wg_s[...])) * _dot(x, wu_s[...])
        y_ref[...] = _dot(hid.astype(BF16), wd_s[...])

    @pl.when(i >= na_ref[0])
    def _():
        y_ref[...] = jnp.zeros(y_ref.shape, F32)


def _experts(blk_e, n_act, xs, w_gate, w_up, w_down):
    cap, d = xs.shape
    r = MOE_ROWS
    ff = w_gate.shape[2]
    return pl.pallas_call(
        _expert_kernel,
        grid_spec=pltpu.PrefetchScalarGridSpec(
            num_scalar_prefetch=2,
            grid=(cap // r,),
            in_specs=[pl.BlockSpec((r, d), lambda i, be, na: (i, 0)),
                      pl.BlockSpec((None, d, ff), lambda i, be, na: (be[i], 0, 0)),
                      pl.BlockSpec((None, d, ff), lambda i, be, na: (be[i], 0, 0)),
                      pl.BlockSpec((None, ff, d), lambda i, be, na: (be[i], 0, 0))],
            out_specs=pl.BlockSpec((r, d), lambda i, be, na: (i, 0)),
            scratch_shapes=[pltpu.VMEM((d, ff), BF16), pltpu.VMEM((d, ff), BF16),
                            pltpu.VMEM((ff, d), BF16)]),
        out_shape=jax.ShapeDtypeStruct((cap, d), F32),
        compiler_params=_cparams(1),
        name="moe_experts",
    )(blk_e, n_act, xs, w_gate, w_up, w_down)


def _combine_kernel(dest_ref, ys_ref, xl_ref, ew_ref, mod_ref, g_ref, out_ref, buf, sems, *, tiles_per_batch):
    i = pl.program_id(0)
    n_steps = pl.num_programs(0)
    t = xl_ref.shape[0]

    def gather(tile, slot):
        def start(r, carry):
            base = (tile * t + r) * 2
            _row_copy(ys_ref, dest_ref[base], buf.at[slot, 0], r, sems.at[slot]).start()
            _row_copy(ys_ref, dest_ref[base + 1], buf.at[slot, 1], r, sems.at[slot]).start()
            return carry

        lax.fori_loop(0, t, start, 0, unroll=DMA_UNROLL)

    @pl.when(i == 0)
    def _():
        gather(0, 0)

    @pl.when(i + 1 < n_steps)
    def _():
        gather(i + 1, (i + 1) % 2)

    slot = i % 2

    def wait(r, carry):
        _row_copy(ys_ref, 0, buf.at[slot, 0], 0, sems.at[slot]).wait()
        _row_copy(ys_ref, 0, buf.at[slot, 1], 0, sems.at[slot]).wait()
        return carry

    lax.fori_loop(0, t, wait, 0, unroll=DMA_UNROLL)
    b = i // tiles_per_batch
    ew = ew_ref[...]
    ffn = buf[slot, 0] * ew[:, 0:1] + buf[slot, 1] * ew[:, 1:2]
    gt2 = mod_ref[pl.ds(b, 1), pl.ds(5 * D_MODEL, D_MODEL)]
    out_ref[...] = xl_ref[...] + gt2 * _rms(ffn, g_ref[...])


def _combine(dest_flat, ys, xl, ew, mod, g, tiles_per_batch):
    n, d = xl.shape
    t = TOK_TILE
    return pl.pallas_call(
        functools.partial(_combine_kernel, tiles_per_batch=tiles_per_batch),
        grid_spec=pltpu.PrefetchScalarGridSpec(
            num_scalar_prefetch=1,
            grid=(n // t,),
            in_specs=[pl.BlockSpec(memory_space=pl.ANY),
                      pl.BlockSpec((t, d), lambda i, dest: (i, 0)),
                      pl.BlockSpec((t, LANES), lambda i, dest: (i, 0)),
                      pl.BlockSpec(mod.shape, lambda i, dest: (0, 0)),
                      pl.BlockSpec(g.shape, lambda i, dest: (0, 0))],
            out_specs=pl.BlockSpec((t, d), lambda i, dest: (i, 0)),
            scratch_shapes=[pltpu.VMEM((2, 2, t, d), F32), pltpu.SemaphoreType.DMA((2,))]),
        out_shape=jax.ShapeDtypeStruct((n, d), F32),
        compiler_params=_cparams(1),
        name="moe_combine",
    )(dest_flat, ys, xl, ew, mod, g)


def kernel(x, c, ctx, c_ctx, w_ada, b_ada, g_pre_mix, g_post_mix, g_pre_ffn, g_post_ffn, w_in,
           conv_w, conv_b, dt_bias, a_log, d_skip, g_ssm_norm, hg_lb, g_hg_norm, w_branch_ssm,
           w_branch_hg, w_out, w_group_router, b_group_router, w_expert_router, b_expert_router,
           w_gate, w_up, w_down):
    nb, n_lat, d = x.shape
    n_ctx = ctx.shape[1]
    rows = n_lat // GRID_W
    assert w_ada.shape[0] == 1 and d == D_MODEL and nb + 1 <= SUBLANES
    assert rows % HG_CHUNK == 0 and n_ctx % (2 * HG_CHUNK) == 0

    w = w_in[0]
    o0 = 0
    w_z = w[:, o0:o0 + SSM_D_INNER]; o0 += SSM_D_INNER
    w_xbc = w[:, o0:o0 + SSM_CONV_DIM].astype(BF16); o0 += SSM_CONV_DIM
    w_dtr = w[:, o0:o0 + 2 * SSM_HEADS]; o0 += 2 * SSM_HEADS
    w_q = w[:, o0:o0 + HG_KEY].astype(BF16); o0 += HG_KEY
    w_f = jnp.stack([w[:, o0:o0 + HG_KEY], w[:, o0 + HG_KEY:o0 + 2 * HG_KEY]]).astype(BF16); o0 += 2 * HG_KEY
    w_i = w[:, o0:o0 + HG_KEY].astype(BF16); o0 += HG_KEY
    w_zg = jnp.concatenate([w_z, w[:, o0:o0 + 3 * d]], axis=1).astype(BF16)
    pad = jnp.zeros((d, LANES - SSM_HEADS), F32)
    w_dt = jnp.concatenate([w_dtr[:, :SSM_HEADS], pad, w_dtr[:, SSM_HEADS:], pad], axis=1).astype(BF16)
    zpad = jnp.zeros((LANES - SSM_HEADS,), F32)
    dtb = jnp.concatenate([dt_bias[0, 0], zpad, dt_bias[0, 1], zpad]).reshape(1, 2 * LANES)
    a_log_p = jnp.pad(a_log[0], ((0, 0), (0, LANES - SSM_HEADS))).reshape(2, 1, LANES)
    dskip_e = jnp.repeat(d_skip[0], SSM_HEAD_DIM, axis=-1).reshape(2, 1, SSM_D_INNER)
    head_of_lane = jnp.arange(SSM_D_INNER, dtype=I32) // SSM_HEAD_DIM
    expand = (jnp.arange(LANES, dtype=I32)[:, None] == head_of_lane[None, :]).astype(BF16)
    lb = jax.nn.softmax(hg_lb.astype(F32), axis=0)[0].reshape(2, 1, HG_KEY)
    w_r = jnp.concatenate([w_group_router[0],
                           jnp.transpose(w_expert_router[0], (1, 0, 2)).reshape(d, MOE_EXPERTS),
                           jnp.zeros((d, LANES - MOE_GROUPS - MOE_EXPERTS), F32)], axis=1)
    b_r = jnp.concatenate([b_group_router[0], b_expert_router[0].reshape(-1),
                           jnp.zeros((LANES - MOE_GROUPS - MOE_EXPERTS,), F32)]).reshape(1, LANES)
    g_hg = g_hg_norm[0].reshape(1, HG_DIM)

    cc = jnp.concatenate([c, c_ctx[None], jnp.zeros((SUBLANES - nb - 1, d), F32)], axis=0)
    mod = _modulation(cc, w_ada[0], b_ada[0])
    h_all, h_lat = _pre_norm(x, ctx, mod, g_pre_mix)

    xbc, dt = _ssd_inputs(h_all, w_xbc, w_dt, conv_w[0], conv_b, dtb)
    y = _ssd_scan(xbc, dt, a_log_p, dskip_e, expand)

    s0 = jnp.zeros((nb, 2, HG_HEADS, HG_DIM, HG_DIM), F32)
    _, s_ctx = _hg_scan(h_all[:, :n_ctx].astype(F32), 1, w_q, w_f, w_i, lb, s0)
    o_grid, _ = _hg_scan(h_lat.reshape(nb, rows, GRID_W, d), HG_COLS_PER_STEP, w_q, w_f, w_i, lb, s_ctx)
    o = o_grid.reshape(2, nb, n_lat, d)

    xl, h2, eid, ew = _merge(x, y, o, h_all, w_zg, w_branch_ssm[0].astype(BF16), w_branch_hg[0].astype(BF16),
                             w_out[0].astype(BF16), g_ssm_norm, g_hg, g_post_mix, g_pre_ffn, mod,
                             w_r, b_r)

    n = nb * n_lat
    cap = -(-(2 * n + MOE_EXPERTS * MOE_ROWS) // MOE_ROWS) * MOE_ROWS
    n_blocks = cap // MOE_ROWS
    dest, blk = _rank(eid.reshape(n, LANES), n_blocks)
    dest_flat = dest[:, 0:2].reshape(-1)
    xs = _dispatch(dest_flat, h2.reshape(n, d), cap)
    ys = _experts(blk[:n_blocks, 0], blk[0:1, 1], xs, w_gate[0], w_up[0], w_down[0])
    out = _combine(dest_flat, ys, xl.reshape(n, d), ew.reshape(n, LANES), mod, g_post_ffn,
                   n_lat // TOK_TILE)
    return out.reshape(nb, n_lat, d)
```

```python
import functools

import jax
import jax.numpy as jnp
from jax import lax
from jax.experimental import pallas as pl
from jax.experimental.pallas import tpu as pltpu

F32 = jnp.float32
BF16 = jnp.bfloat16
I32 = jnp.int32
HIGHEST = lax.Precision.HIGHEST

D_MODEL = 1024
GRID_W = 64
RMS_EPS = 1e-6
SSM_D_INNER = 2048
SSM_HEAD_DIM = 64
SSM_HEADS = 32
SSM_GROUPS = 4
SSM_HEADS_PER_GROUP = 8
SSM_STATE = 128
SSM_GROUP_WIDTH = SSM_HEADS_PER_GROUP * SSM_HEAD_DIM
SSM_CONV_DIM = SSM_D_INNER + 2 * SSM_GROUPS * SSM_STATE
HG_HEADS = 8
HG_DIM = 128
HG_KEY = HG_HEADS * HG_DIM
MOE_GROUPS = 4
MOE_EXPERTS_PER_GROUP = 8
MOE_EXPERTS = 32
MOE_D_FF = 512

LANES = 128
SUBLANES = 8
VMEM_LIMIT_BYTES = 56 * 1024 * 1024

TOK_TILE = 256
SSD_CHUNK = 128
HG_CHUNK = 64
HG_SUB = 16
HG_FACTOR_LOG2_RANGE = 80.0
HG_COLS_PER_STEP = 8
HG_COLS_PER_PASS = 4
MOE_ROWS = 256
RANK_TILE = 512
DMA_UNROLL = 8


def _cparams(n_axes):
    return pltpu.CompilerParams(dimension_semantics=("arbitrary",) * n_axes,
                                vmem_limit_bytes=VMEM_LIMIT_BYTES)


def _sigmoid(x):
    return 0.5 * jnp.tanh(0.5 * x) + 0.5


def _silu(x):
    return x * _sigmoid(x)


def _rms(v, g):
    ms = jnp.mean(v * v, axis=-1, keepdims=True)
    return v * lax.rsqrt(ms + RMS_EPS) * g


def _dot(a, b):
    return jnp.dot(a, b, preferred_element_type=F32)


def _dot_nt(a, b):
    return lax.dot_general(a, b, (((1,), (1,)), ((), ())), preferred_element_type=F32)


def _dot_tn(a, b):
    return lax.dot_general(a, b, (((0,), (0,)), ((), ())), preferred_element_type=F32)


def _mod_kernel(c_ref, w_ref, b_ref, o_ref):
    s = _silu(c_ref[...])
    o_ref[...] = jnp.dot(s, w_ref[...], precision=HIGHEST, preferred_element_type=F32) + b_ref[...]


def _modulation(cc, w_ada, b_ada):
    n = w_ada.shape[1]
    tn = 1024
    return pl.pallas_call(
        _mod_kernel,
        grid=(n // tn,),
        in_specs=[pl.BlockSpec((SUBLANES, D_MODEL), lambda j: (0, 0)),
                  pl.BlockSpec((D_MODEL, tn), lambda j: (0, j)),
                  pl.BlockSpec((1, tn), lambda j: (0, j))],
        out_specs=pl.BlockSpec((SUBLANES, tn), lambda j: (0, j)),
        out_shape=jax.ShapeDtypeStruct((SUBLANES, n), F32),
        compiler_params=_cparams(1),
        name="adaln_mod",
    )(cc, w_ada, b_ada.reshape(1, n))


def _norm_kernel(x_ref, ctx_ref, mod_ref, g_ref, all_ref, lat_ref, *, ctx_row):
    b = pl.program_id(0)
    i = pl.program_id(1)
    g = g_ref[...]

    def nm(v, row):
        sh = mod_ref[pl.ds(row, 1), pl.ds(0, D_MODEL)]
        sc = mod_ref[pl.ds(row, 1), pl.ds(D_MODEL, D_MODEL)]
        return _rms(v, g) * (1.0 + sc) + sh

    @pl.when(i == 0)
    def _():
        all_ref[...] = nm(ctx_ref[...], ctx_row).astype(BF16)

    @pl.when(i > 0)
    def _():
        h = nm(x_ref[...], b)
        all_ref[...] = h.astype(BF16)
        lat_ref[...] = h


def _pre_norm(x, ctx, mod, g):
    nb, n_lat, d = x.shape
    n_ctx = ctx.shape[1]
    assert n_ctx == TOK_TILE and n_lat % TOK_TILE == 0
    nt = n_lat // TOK_TILE
    return pl.pallas_call(
        functools.partial(_norm_kernel, ctx_row=nb),
        grid=(nb, nt + 1),
        in_specs=[pl.BlockSpec((None, TOK_TILE, d), lambda b, i: (b, jnp.maximum(i - 1, 0), 0)),
                  pl.BlockSpec((None, TOK_TILE, d), lambda b, i: (b, 0, 0)),
                  pl.BlockSpec(mod.shape, lambda b, i: (0, 0)),
                  pl.BlockSpec((1, d), lambda b, i: (0, 0))],
        out_specs=[pl.BlockSpec((None, TOK_TILE, d), lambda b, i: (b, i, 0)),
                   pl.BlockSpec((None, TOK_TILE, d), lambda b, i: (b, jnp.maximum(i - 1, 0), 0))],
        out_shape=[jax.ShapeDtypeStruct((nb, n_ctx + n_lat, d), BF16),
                   jax.ShapeDtypeStruct((nb, n_lat, d), F32)],
        compiler_params=_cparams(2),
        name="pre_norm",
    )(x, ctx, mod, g)


XBC_COL_TILE = 512


def _xbc_kernel(h_ref, hp_ref, hn_ref, w_ref, wdt_ref, cw_ref, cb_ref, dtb_ref, xbc_ref, dt_ref,
                raw_s, *, n_tiles):
    i = pl.program_id(1)
    h = h_ref[...]
    tm = h.shape[0]
    halo = hp_ref.shape[0]
    prev_ok = (i >= 2).astype(BF16)
    next_ok = jnp.logical_and(i >= 1, i < n_tiles - 1).astype(BF16)
    rows = jnp.concatenate([hp_ref[...] * prev_ok, h, hn_ref[...] * next_ok], axis=0)
    for c in range(SSM_CONV_DIM // XBC_COL_TILE):
        sl = slice(c * XBC_COL_TILE, (c + 1) * XBC_COL_TILE)
        raw_s[...] = _dot(rows, w_ref[:, sl])
        up = raw_s[halo - 1:halo - 1 + tm, :]
        mid = raw_s[halo:halo + tm, :]
        dn = raw_s[halo + 1:halo + 1 + tm, :]
        conv = cw_ref[0:1, sl] * up + cw_ref[1:2, sl] * mid + cw_ref[2:3, sl] * dn + cb_ref[:, sl]
        xbc_ref[:, sl] = _silu(conv).astype(BF16)
    u = _dot(h, wdt_ref[...]) + dtb_ref[...]
    dt_ref[...] = jnp.maximum(u, 0.0) + jnp.log(1.0 + jnp.exp(-jnp.abs(u)))


def _ssd_inputs(h_all, w_xbc, w_dt, conv_w, conv_b, dt_bias):
    nb, ltot, d = h_all.shape
    nt = ltot // TOK_TILE
    halo = 2 * SUBLANES
    n_halo = ltot // halo
    per = TOK_TILE // halo
    return pl.pallas_call(
        functools.partial(_xbc_kernel, n_tiles=nt),
        grid=(nb, nt),
        in_specs=[pl.BlockSpec((None, TOK_TILE, d), lambda b, i: (b, i, 0)),
                  pl.BlockSpec((None, halo, d), lambda b, i: (b, jnp.maximum(i * per - 1, 0), 0)),
                  pl.BlockSpec((None, halo, d), lambda b, i: (b, jnp.minimum((i + 1) * per, n_halo - 1), 0)),
                  pl.BlockSpec(w_xbc.shape, lambda b, i: (0, 0)),
                  pl.BlockSpec(w_dt.shape, lambda b, i: (0, 0)),
                  pl.BlockSpec(conv_w.shape, lambda b, i: (0, 0)),
                  pl.BlockSpec(conv_b.shape, lambda b, i: (0, 0)),
                  pl.BlockSpec(dt_bias.shape, lambda b, i: (0, 0))],
        out_specs=[pl.BlockSpec((None, TOK_TILE, SSM_CONV_DIM), lambda b, i: (b, i, 0)),
                   pl.BlockSpec((None, TOK_TILE, 2 * LANES), lambda b, i: (b, i, 0))],
        out_shape=[jax.ShapeDtypeStruct((nb, ltot, SSM_CONV_DIM), BF16),
                   jax.ShapeDtypeStruct((nb, ltot, 2 * LANES), F32)],
        scratch_shapes=[pltpu.VMEM((TOK_TILE + 2 * halo, XBC_COL_TILE), F32)],
        compiler_params=_cparams(2),
        name="ssd_inputs",
    )(h_all, h_all, h_all, w_xbc, w_dt, conv_w, conv_b, dt_bias)


def _ssd_kernel(xbc_ref, dt_ref, alog_ref, dskip_ref, e_ref, y_ref, state):
    d = pl.program_id(1)
    step = pl.program_id(2)
    q = SSD_CHUNK

    @pl.when(step == 0)
    def _():
        state[...] = jnp.zeros(state.shape, F32)

    lane = lax.broadcasted_iota(I32, (1, LANES), 1)
    a = jnp.where(lane < SSM_HEADS, -jnp.exp(alog_ref[...]), 0.0)
    dt = dt_ref[...]
    adt = dt * a
    r = lax.broadcasted_iota(I32, (q, q), 0)
    c = lax.broadcasted_iota(I32, (q, q), 1)
    mask = jnp.where(d == 0, r - c, c - r) >= 0
    cs = jnp.dot(mask.astype(F32), adt, precision=HIGHEST, preferred_element_type=F32)
    tot = jnp.sum(adt, axis=0, keepdims=True)
    cs_t = cs.T
    lhs = jnp.concatenate([dt, jnp.exp(cs), jnp.exp(tot - cs),
                           jnp.broadcast_to(jnp.exp(tot), (SUBLANES, LANES))], axis=0).astype(BF16)
    lane_h = lax.broadcasted_iota(I32, (q, LANES), 1)
    for g in range(SSM_GROUPS):
        gs = slice(g * SSM_GROUP_WIDTH, (g + 1) * SSM_GROUP_WIDTH)
        ex = _dot(lhs, e_ref[:, gs])
        dt_e, e1_e, e2_e, etot_e = ex[0:q], ex[q:2 * q], ex[2 * q:3 * q], ex[3 * q:3 * q + 1]
        x = xbc_ref[:, gs].astype(F32)
        bg = xbc_ref[:, SSM_D_INNER + g * SSM_STATE:SSM_D_INNER + (g + 1) * SSM_STATE]
        cg = xbc_ref[:, SSM_D_INNER + (SSM_GROUPS + g) * SSM_STATE:
                     SSM_D_INNER + (SSM_GROUPS + g + 1) * SSM_STATE]
        xdt = x * dt_e
        cb = _dot_nt(cg, bg)
        s_in = state[g]
        y_off = _dot(cg, s_in.astype(BF16)) * e1_e
        state[g] = s_in * etot_e + _dot_tn(bg, (xdt * e2_e).astype(BF16))
        parts = []
        for m in range(SSM_HEADS_PER_GROUP // 2):
            xp = xdt[:, m * LANES:(m + 1) * LANES]
            acc = None
            for half in range(2):
                hd = g * SSM_HEADS_PER_GROUP + 2 * m + half
                col = cs[:, hd:hd + 1]
                rowv = cs_t[hd:hd + 1, :]
                decay = jnp.exp(jnp.minimum(col - rowv, 0.0))
                w = jnp.where(mask, cb * decay, 0.0).astype(BF16)
                sel = (lane_h < SSM_HEAD_DIM) if half == 0 else (lane_h >= SSM_HEAD_DIM)
                t = _dot(w, jnp.where(sel, xp, 0.0).astype(BF16))
                acc = t if acc is None else acc + t
            parts.append(acc)
        y = jnp.concatenate(parts, axis=1) + y_off + x * dskip_ref[:, gs]
        y_ref[:, gs] = y.astype(BF16)


def _ssd_scan(xbc, dt, a_log_p, dskip_e, expand):
    nb, ltot, _ = xbc.shape
    q = SSD_CHUNK
    nc = ltot // q
    n_ctx_chunks = TOK_TILE // q

    def chunk(d, s):
        rev = jnp.where(s < n_ctx_chunks, n_ctx_chunks - 1 - s, nc + n_ctx_chunks - 1 - s)
        return jnp.where(d == 0, s, rev)

    return pl.pallas_call(
        _ssd_kernel,
        grid=(nb, 2, nc),
        in_specs=[pl.BlockSpec((None, q, SSM_CONV_DIM), lambda b, d, s: (b, chunk(d, s), 0)),
                  pl.BlockSpec((None, q, LANES), lambda b, d, s: (b, chunk(d, s), d)),
                  pl.BlockSpec((None, 1, LANES), lambda b, d, s: (d, 0, 0)),
                  pl.BlockSpec((None, 1, SSM_D_INNER), lambda b, d, s: (d, 0, 0)),
                  pl.BlockSpec(expand.shape, lambda b, d, s: (0, 0))],
        out_specs=pl.BlockSpec((None, None, q, SSM_D_INNER), lambda b, d, s: (d, b, chunk(d, s), 0)),
        out_shape=jax.ShapeDtypeStruct((2, nb, ltot, SSM_D_INNER), BF16),
        scratch_shapes=[pltpu.VMEM((SSM_GROUPS, SSM_STATE, SSM_GROUP_WIDTH), F32)],
        compiler_params=_cparams(3),
        name="ssd_scan",
    )(xbc, dt, a_log_p, dskip_e, expand)


def _hg_intra(q_s, k_s, b_s, r0, off):
    c = HG_CHUNK
    cols = pl.ds(off, HG_DIM)
    lane8 = lax.broadcasted_iota(I32, (SUBLANES, c), 1)
    lane_sub = lax.broadcasted_iota(I32, (HG_SUB, c), 1)
    rows = []
    for i in range(c // HG_SUB):
        lo = r0 + i * HG_SUB
        bi = b_s[lo:lo + HG_SUB, cols]
        qi = q_s[lo:lo + HG_SUB, cols]
        top = jnp.zeros((SUBLANES, c), F32)
        bot = jnp.zeros((SUBLANES, c), F32)
        for j in range(HG_SUB):
            s = lo + j
            at = i * HG_SUB + j
            brow = b_s[s:s + 1, cols]
            krow = k_s[s:s + 1, cols]
            if j < SUBLANES:
                p = qi * krow * jnp.exp2(bi - brow)
                top = jnp.where(lane8 == at, jnp.sum(p[0:SUBLANES], axis=1, keepdims=True), top)
                bot = jnp.where(lane8 == at, jnp.sum(p[SUBLANES:], axis=1, keepdims=True), bot)
            else:
                p = qi[SUBLANES:] * krow * jnp.exp2(bi[SUBLANES:] - brow)
                bot = jnp.where(lane8 == at, jnp.sum(p, axis=1, keepdims=True), bot)
        a_row = jnp.concatenate([top, bot], axis=0)
        if i > 0:
            bref = b_s[lo - 1:lo, cols]
            qt = qi * jnp.exp2(bi - bref)
            kt = k_s[r0:lo, cols] * jnp.exp2(bref - b_s[r0:lo, cols])
            ktp = jnp.concatenate([kt, jnp.zeros((c - i * HG_SUB, HG_DIM), F32)], axis=0)
            a_off = _dot_nt(qt.astype(BF16), ktp.astype(BF16))
            a_row = jnp.where(lane_sub < i * HG_SUB, a_off, a_row)
        rows.append(a_row)
    a = jnp.concatenate(rows, axis=0)
    r64 = lax.broadcasted_iota(I32, (c, c), 0)
    c64 = lax.broadcasted_iota(I32, (c, c), 1)
    return jnp.where(r64 >= c64, a, 0.0)


def _hg_kernel(h_ref, wq_ref, wf_ref, wi_ref, lb_ref, sin_ref, o_ref, sout_ref,
               pin_s, pout_s, q_s, k_s, b_s, v_s, qe_s, kd_s, ki_s, dec_s, o_s, state, *, log2_rows,
               log2_ncol, pass_rows):
    d = pl.program_id(1)
    step = pl.program_id(2)
    nsteps = pl.num_programs(2)
    t_all = 1 << (log2_rows + log2_ncol)
    c = HG_CHUNK
    nch = pass_rows // c

    def source_row(p):
        src = ((p & ((1 << log2_rows) - 1)) << log2_ncol) + (p >> log2_rows)
        return jnp.where(d == 0, src, t_all - 1 - src)

    @pl.when(step == 0)
    def _():
        state[...] = sin_ref[...]
        shape = (t_all, t_all)
        pin_s[...] = (lax.broadcasted_iota(I32, shape, 1)
                      == source_row(lax.broadcasted_iota(I32, shape, 0))).astype(BF16)
        pout_s[...] = (lax.broadcasted_iota(I32, shape, 0)
                       == source_row(lax.broadcasted_iota(I32, shape, 1))).astype(BF16)

    r64 = lax.broadcasted_iota(I32, (c, c), 0)
    c64 = lax.broadcasted_iota(I32, (c, c), 1)
    tril = (r64 >= c64).astype(BF16)
    h_rows =h_ref[...].reshape(t_all, HG_KEY).astype(BF16)

    for ps in range(t_all // pass_rows):
        prs = slice(ps * pass_rows, (ps + 1) * pass_rows)
        hb = _dot(pin_s[prs, :], h_rows).astype(BF16)
        q_s[...] = _silu(_dot(hb, wq_ref[...]))
        fz = _dot(hb, wf_ref[...])
        t = jnp.exp(-jnp.abs(fz))
        r = 1.0 / (1.0 + t)
        big, small = r, t * r
        pos = fz >= 0.0
        lb = lb_ref[...]
        lf2 = jnp.log2(lb + (1.0 - lb) * jnp.where(pos, big, small))
        k_s[...] = (1.0 - lb) * jnp.where(pos, small, big)
        v_s[...] = _dot(hb, wi_ref[...]).astype(BF16)
        for ch in range(nch):
            rs = slice(ch * c, (ch + 1) * c)
            x = lf2[rs]
            hi = x.astype(BF16)
            rem = x - hi.astype(F32)
            mid = rem.astype(BF16)
            low = (rem - mid.astype(F32)).astype(BF16)
            b = _dot(tril, hi) + _dot(tril, mid) + _dot(tril, low)
            b_s[rs, :] = b
            btot = b[c - 1:c]
            qe_s[rs, :] = (q_s[rs, :] * jnp.exp2(b)).astype(BF16)
            kd_s[rs, :] = (k_s[rs, :] * jnp.exp2(btot - b)).astype(BF16)
            ki_s[rs, :] = (k_s[rs, :] * jnp.exp2(-b)).astype(BF16)
            dec_s[ch:ch + 1, :] = jnp.exp2(btot)
            steepest = -btot if ch == 0 else jnp.maximum(steepest, -btot)
        bounded = jnp.max(steepest) <= HG_FACTOR_LOG2_RANGE

        def head(hh, carry, factored):
            off = pl.multiple_of(hh * HG_DIM, HG_DIM)
            cols = pl.ds(off, HG_DIM)
            s_t = state[hh]
            for ch in range(nch):
                rs = slice(ch * c, (ch + 1) * c)
                if factored:
                    a = jnp.where(r64 >= c64, _dot_nt(qe_s[rs, cols], ki_s[rs, cols]), 0.0)
                else:
                    a = _hg_intra(q_s, k_s, b_s, ch * c, off)
                v = v_s[rs, cols]
                o_s[rs, cols] = _dot_nt(qe_s[rs, cols], s_t.astype(BF16)) + _dot(a.astype(BF16), v)
                s_t = s_t * dec_s[ch:ch + 1, cols] + _dot_tn(v, kd_s[rs, cols])
            state[hh] = s_t
            return carry

        @pl.when(bounded)
        def _():
            lax.fori_loop(0, HG_HEADS, functools.partial(head, factored=True), 0)

        @pl.when(jnp.logical_not(bounded))
        def _():
            lax.fori_loop(0, HG_HEADS, functools.partial(head, factored=False), 0)
        back = _dot(pout_s[:, prs], o_s[...].astype(BF16)).reshape(o_ref.shape)
        if ps == 0:
            o_ref[...] = back
        else:
            o_ref[...] += back

    @pl.when(step == nsteps - 1)
    def _():
        sout_ref[...] = state[...]


def _hg_scan(h_grid, ncol, wq, wf, wi, lb, s_in):
    if h_grid.ndim == 3:
        nb, rows, d_model = h_grid.shape
        ncol_total = 1
    else:
        nb, rows, ncol_total, d_model = h_grid.shape
    nsteps = ncol_total // ncol
    t_all = rows * ncol
    pass_rows = min(t_all, HG_COLS_PER_PASS * LANES)
    log2_rows, log2_ncol = rows.bit_length() - 1, ncol.bit_length() - 1
    assert rows == 1 << log2_rows and ncol == 1 << log2_ncol and t_all % pass_rows == 0
    assert pass_rows // HG_CHUNK <= SUBLANES

    def pos(d, s):
        return jnp.where(d == 0, s, nsteps - 1 - s)

    once = pl.Buffered(1)
    if h_grid.ndim == 3:
        h_spec = pl.BlockSpec((None, rows, d_model), lambda b, d, s: (b, 0, 0))
        o_spec = pl.BlockSpec((None, None, rows, d_model), lambda b, d, s: (d, b, 0, 0))
    else:
        h_spec = pl.BlockSpec((None, rows, ncol, d_model), lambda b, d, s: (b, 0, pos(d, s), 0))
        o_spec = pl.BlockSpec((None, None, rows, ncol, d_model), lambda b, d, s: (d, b, 0, pos(d, s), 0))
    st_spec = pl.BlockSpec((None, None, HG_HEADS, HG_DIM, HG_DIM), lambda b, d, s: (b, d, 0, 0, 0))
    wide_f32 = pltpu.VMEM((pass_rows, HG_KEY), F32)
    wide_bf16 = pltpu.VMEM((pass_rows, HG_KEY), BF16)
    perm = pltpu.VMEM((t_all, t_all), BF16)
    return pl.pallas_call(
        functools.partial(_hg_kernel, log2_rows=log2_rows, log2_ncol=log2_ncol, pass_rows=pass_rows),
        grid=(nb, 2, nsteps),
        in_specs=[h_spec,
                  pl.BlockSpec(wq.shape, lambda b, d, s: (0, 0), pipeline_mode=once),
                  pl.BlockSpec((None, d_model, HG_KEY), lambda b, d, s: (d, 0, 0), pipeline_mode=once),
                  pl.BlockSpec(wi.shape, lambda b, d, s: (0, 0), pipeline_mode=once),
                  pl.BlockSpec((None, 1, HG_KEY), lambda b, d, s: (d, 0, 0)),
                  st_spec],
        out_specs=[o_spec, st_spec],
        out_shape=[jax.ShapeDtypeStruct((2,) + h_grid.shape, F32),
                   jax.ShapeDtypeStruct(s_in.shape, F32)],
        scratch_shapes=[perm, perm, wide_f32, wide_f32, wide_f32, wide_bf16, wide_bf16, wide_bf16, wide_bf16,
                        pltpu.VMEM((SUBLANES, HG_KEY), F32), wide_f32,
                        pltpu.VMEM((HG_HEADS, HG_DIM, HG_DIM), F32)],
        compiler_params=_cparams(3),
        name="hgrn2_scan",
    )(h_grid, wq, wf, wi, lb, s_in)


def _merge_kernel(x_ref, yf_ref, yb_ref, of_ref, ob_ref, h_ref, wzg_ref, wbs_ref, wbh_ref, wo_ref,
                  gs_ref, gh_ref, gpm_ref, gpf_ref, mod_ref, wr_ref, br_ref,
                  xl_ref, h2_ref, eid_ref, ew_ref):
    b = pl.program_id(0)
    tm = x_ref.shape[0]
    d = D_MODEL
    h = h_ref[...]
    z = _dot(h, wzg_ref[:, 0:SSM_D_INNER])
    yz = (yf_ref[...].astype(F32) + yb_ref[...].astype(F32)) * _silu(z)
    parts = []
    for g in range(SSM_GROUPS):
        gsl = slice(g * SSM_GROUP_WIDTH, (g + 1) * SSM_GROUP_WIDTH)
        parts.append(_rms(yz[:, gsl], gs_ref[:, gsl]).astype(BF16))
    br_ssm = _dot(jnp.concatenate(parts, axis=1), wbs_ref[...])

    o = of_ref[...].astype(F32) + ob_ref[...].astype(F32)
    og = _dot(h, wzg_ref[:, SSM_D_INNER:SSM_D_INNER + d])
    parts = []
    for hh in range(HG_HEADS):
        hsl = slice(hh * HG_DIM, (hh + 1) * HG_DIM)
        parts.append((_rms(o[:, hsl], gh_ref[...]) * _silu(og[:, hsl])).astype(BF16))
    br_hg = _dot(jnp.concatenate(parts, axis=1), wbh_ref[...])

    gm = _dot(h, wzg_ref[:, SSM_D_INNER + d:SSM_D_INNER + 2 * d])
    gh = _dot(h, wzg_ref[:, SSM_D_INNER + 2 * d:SSM_D_INNER + 3 * d])
    mixed = _sigmoid(gm) * br_ssm + _sigmoid(gh) * br_hg
    mix = _dot(mixed.astype(BF16), wo_ref[...])

    gt1 = mod_ref[pl.ds(b, 1), pl.ds(2 * d, d)]
    sh2 = mod_ref[pl.ds(b, 1), pl.ds(3 * d, d)]
    sc2 = mod_ref[pl.ds(b, 1), pl.ds(4 * d, d)]
    xl = x_ref[...] + gt1 * _rms(mix, gpm_ref[...])
    xl_ref[...] = xl
    h2 = _rms(xl, gpf_ref[...]) * (1.0 + sc2) + sh2
    h2_ref[...] = h2

    logits = jnp.dot(h2, wr_ref[...], precision=HIGHEST, preferred_element_type=F32) + br_ref[...]
    lane = lax.broadcasted_iota(I32, (tm, LANES), 1)
    lane_f = lane.astype(F32)
    neg = jnp.float32(-jnp.inf)
    far = jnp.float32(2 * LANES)
    gl = jnp.where(lane < MOE_GROUPS, logits, neg)
    gmax = jnp.max(gl, axis=1, keepdims=True)
    p_grp = 1.0 / jnp.sum(jnp.exp(gl - gmax), axis=1, keepdims=True)
    grp = jnp.min(jnp.where(gl == gmax, lane_f, far), axis=1, keepdims=True)
    lo = MOE_GROUPS + MOE_EXPERTS_PER_GROUP * grp
    el = jnp.where(jnp.logical_and(lane_f >= lo, lane_f < lo + MOE_EXPERTS_PER_GROUP), logits, neg)
    m1 = jnp.max(el, axis=1, keepdims=True)
    i1 = jnp.min(jnp.where(el == m1, lane_f, far), axis=1, keepdims=True)
    el2 = jnp.where(lane_f == i1, neg, el)
    m2 = jnp.max(el2, axis=1, keepdims=True)
    i2 = jnp.min(jnp.where(el2 == m2, lane_f, far), axis=1, keepdims=True)
    e2 = jnp.exp(m2 - m1)
    w1 = p_grp / (1.0 + e2)
    w2 = p_grp * e2 / (1.0 + e2)
    eid = jnp.where(lane == 0, i1, i2) - float(MOE_GROUPS)
    eid_ref[...] = jnp.where(lane < 2, eid, 0.0).astype(I32)
    ew_ref[...] = jnp.where(lane == 0, w1, jnp.where(lane == 1, w2, 0.0))


def _merge(x, y, o, h_all, w_zg, w_bs, w_bh, w_o, g_ssm, g_hg, g_pm, g_pf, mod, w_r, b_r):
    nb, n_lat, d = x.shape
    tm = TOK_TILE
    ctx_tiles = (y.shape[2] - n_lat) // tm

    def full(a):
        return pl.BlockSpec(a.shape, lambda b, i: (0,) * a.ndim, pipeline_mode=pl.Buffered(1))

    tok = lambda w: pl.BlockSpec((None, tm, w), lambda b, i: (b, i, 0))
    return pl.pallas_call(
        _merge_kernel,
        grid=(nb, n_lat // tm),
        in_specs=[tok(d),
                  pl.BlockSpec((None, None, tm, SSM_D_INNER), lambda b, i: (0, b, i + ctx_tiles, 0)),
                  pl.BlockSpec((None, None, tm, SSM_D_INNER), lambda b, i: (1, b, i + ctx_tiles, 0)),
                  pl.BlockSpec((None, None, tm, d), lambda b, i: (0, b, i, 0)),
                  pl.BlockSpec((None, None, tm, d), lambda b, i: (1, b, i, 0)),
                  pl.BlockSpec((None, tm, d), lambda b, i: (b, i + ctx_tiles, 0)),
                  full(w_zg), full(w_bs), full(w_bh), full(w_o), full(g_ssm), full(g_hg), full(g_pm),
                  full(g_pf), full(mod), full(w_r), full(b_r)],
        out_specs=[tok(d), tok(d), tok(LANES), tok(LANES)],
        out_shape=[jax.ShapeDtypeStruct((nb, n_lat, d), F32),
                   jax.ShapeDtypeStruct((nb, n_lat, d), F32),
                   jax.ShapeDtypeStruct((nb, n_lat, LANES), I32),
                   jax.ShapeDtypeStruct((nb, n_lat, LANES), F32)],
        compiler_params=_cparams(2),
        name="merge_route",
    )(x, y, y, o, o, h_all, w_zg, w_bs, w_bh, w_o, g_ssm, g_hg, g_pm, g_pf, mod, w_r, b_r)


def _rank_kernel(eid_ref, dest_ref, blk_ref, cnt, base):
    p = pl.program_id(0)
    i = pl.program_id(1)
    t = eid_ref.shape[0]
    lane = lax.broadcasted_iota(I32, (t, LANES), 1)
    e = eid_ref[...]
    oh0 = (lane == e[:, 0:1]).astype(F32)
    oh1 = (lane == e[:, 1:2]).astype(F32)
    c0 = jnp.sum(oh0, axis=0, keepdims=True)
    c1 = jnp.sum(oh1, axis=0, keepdims=True)

    @pl.when(jnp.logical_and(p == 0, i == 0))
    def _():
        cnt[...] = jnp.zeros(cnt.shape, F32)

    @pl.when(p == 0)
    def _():
        cnt[...] = cnt[...] + c0 + c1

    @pl.when(jnp.logical_and(p == 1, i == 0))
    def _():
        lane1 = lax.broadcasted_iota(I32, (SUBLANES, LANES), 1)
        padded = jnp.floor((cnt[...] + (MOE_ROWS - 1)) / MOE_ROWS) * MOE_ROWS
        padded = jnp.broadcast_to(jnp.where(lane1[0:1] < MOE_EXPERTS, padded, 0.0), (SUBLANES, LANES))
        rr = lax.broadcasted_iota(I32, (LANES, LANES), 0)
        cc = lax.broadcasted_iota(I32, (LANES, LANES), 1)
        start = jnp.dot(padded, (rr < cc).astype(F32), precision=HIGHEST, preferred_element_type=F32)
        base[...] = start[0:1]
        pad_end = start[0:1] + padded[0:1]
        pad_end = jnp.where(lane1[0:1] < MOE_EXPERTS, pad_end, jnp.float32(2.0 ** 30))
        nblk = blk_ref.shape[0]
        bstart = lax.broadcasted_iota(I32, (nblk, LANES), 0).astype(F32) * MOE_ROWS
        owner = jnp.sum((pad_end <= bstart).astype(F32), axis=1, keepdims=True)
        owner = jnp.minimum(owner, MOE_EXPERTS - 1)
        total = jnp.max(jnp.where(lane1[0:1] == MOE_EXPERTS - 1, pad_end, 0.0), axis=1, keepdims=True)
        lane_b = lax.broadcasted_iota(I32, (nblk, LANES), 1)
        blk_ref[...] = jnp.where(lane_b == 0, owner, total / MOE_ROWS).astype(I32)

    @pl.when(p == 1)
    def _():
        rr = lax.broadcasted_iota(I32, (t, t), 0)
        cc = lax.broadcasted_iota(I32, (t, t), 1)
        before = (rr > cc).astype(BF16)
        b0 = base[...]
        r0 = _dot(before, oh0.astype(BF16)) + b0
        r1 = _dot(before, oh1.astype(BF16)) + b0 + c0
        d0 = jnp.sum(oh0 * r0, axis=1, keepdims=True)
        d1 = jnp.sum(oh1 * r1, axis=1, keepdims=True)
        dest_ref[...] = jnp.where(lane == 0, d0, jnp.where(lane == 1, d1, 0.0)).astype(I32)
        base[...] = b0 + c0 + c1


def _rank(eid, n_blocks):
    n = eid.shape[0]
    t = RANK_TILE
    nblk_pad = -(-n_blocks // SUBLANES) * SUBLANES
    return pl.pallas_call(
        _rank_kernel,
        grid=(2, n // t),
        in_specs=[pl.BlockSpec((t, LANES), lambda p, i: (i, 0))],
        out_specs=[pl.BlockSpec((t, LANES), lambda p, i: (p * i, 0)),
                   pl.BlockSpec((nblk_pad, LANES), lambda p, i: (0, 0))],
        out_shape=[jax.ShapeDtypeStruct((n, LANES), I32),
                   jax.ShapeDtypeStruct((nblk_pad, LANES), I32)],
        scratch_shapes=[pltpu.VMEM((1, LANES), F32), pltpu.VMEM((1, LANES), F32)],
        compiler_params=_cparams(2),
        name="moe_rank",
    )(eid)


def _row_copy(src, s, dst, t, sem):
    return pltpu.make_async_copy(src.at[pl.ds(s, 1)], dst.at[pl.ds(t, 1)], sem)


def _dispatch_kernel(dest_ref, h_ref, zin_ref, out_ref, sem):
    del zin_ref
    i = pl.program_id(0)
    t = h_ref.shape[0]

    def start(r, carry):
        base = (i * t + r) * 2
        _row_copy(h_ref, r, out_ref, dest_ref[base], sem).start()
        _row_copy(h_ref, r, out_ref, dest_ref[base + 1], sem).start()
        return carry

    lax.fori_loop(0, t, start, 0, unroll=DMA_UNROLL)

    def wait(r, carry):
        _row_copy(h_ref, 0, out_ref, 0, sem).wait()
        _row_copy(h_ref, 0, out_ref, 0, sem).wait()
        return carry

    lax.fori_loop(0, t, wait, 0, unroll=DMA_UNROLL)


def _dispatch(dest_flat, h2, cap):
    n, d = h2.shape
    t = TOK_TILE
    return pl.pallas_call(
        _dispatch_kernel,
        grid_spec=pltpu.PrefetchScalarGridSpec(
            num_scalar_prefetch=1,
            grid=(n // t,),
            in_specs=[pl.BlockSpec((t, d), lambda i, dest: (i, 0)),
                      pl.BlockSpec(memory_space=pl.ANY)],
            out_specs=pl.BlockSpec(memory_space=pl.ANY),
            scratch_shapes=[pltpu.SemaphoreType.DMA]),
        out_shape=jax.ShapeDtypeStruct((cap, d), F32),
        input_output_aliases={2: 0},
        compiler_params=_cparams(1),
        name="moe_dispatch",
    )(dest_flat, h2, jnp.zeros((cap, d), F32))


def _expert_kernel(be_ref, na_ref, x_ref, wg_ref, wu_ref, wd_ref, y_ref, wg_s, wu_s, wd_s):
    i = pl.program_id(0)
    e = be_ref[i]
    prev = be_ref[jnp.maximum(i - 1, 0)]

    @pl.when(jnp.logical_or(i == 0, e != prev))
    def _():
        wg_s[...] = wg_ref[...].astype(BF16)
        wu_s[...] = wu_ref[...].astype(BF16)
        wd_s[...] = wd_ref[...].astype(BF16)

    @pl.when(i < na_ref[0])
    def _():
        x = x_ref[...].astype(BF16)
        hid = _silu(_dot(x, wg_s[...])) * _dot(x, wu_s[...])
        y_ref[...] = _dot(hid.astype(BF16), wd_s[...])

    @pl.when(i >= na_ref[0])
    def _():
        y_ref[...] = jnp.zeros(y_ref.shape, F32)


def _experts(blk_e, n_act, xs, w_gate, w_up, w_down):
    cap, d = xs.shape
    r = MOE_ROWS
    ff = w_gate.shape[2]
    return pl.pallas_call(
        _expert_kernel,
        grid_spec=pltpu.PrefetchScalarGridSpec(
            num_scalar_prefetch=2,
            grid=(cap // r,),
            in_specs=[pl.BlockSpec((r, d), lambda i, be, na: (i, 0)),
                      pl.BlockSpec((None, d, ff), lambda i, be, na: (be[i], 0, 0)),
                      pl.BlockSpec((None, d, ff), lambda i, be, na: (be[i], 0, 0)),
                      pl.BlockSpec((None, ff, d), lambda i, be, na: (be[i], 0, 0))],
            out_specs=pl.BlockSpec((r, d), lambda i, be, na: (i, 0)),
            scratch_shapes=[pltpu.VMEM((d, ff), BF16), pltpu.VMEM((d, ff), BF16),
                            pltpu.VMEM((ff, d), BF16)]),
        out_shape=jax.ShapeDtypeStruct((cap, d), F32),
        compiler_params=_cparams(1),
        name="moe_experts",
    )(blk_e, n_act, xs, w_gate, w_up, w_down)


def _combine_kernel(dest_ref, ys_ref, xl_ref, ew_ref, mod_ref, g_ref, out_ref, buf, sems, *, tiles_per_batch):
    i = pl.program_id(0)
    n_steps = pl.num_programs(0)
    t = xl_ref.shape[0]

    def gather(tile, slot):
        def start(r, carry):
            base = (tile * t + r) * 2
            _row_copy(ys_ref, dest_ref[base], buf.at[slot, 0], r, sems.at[slot]).start()
            _row_copy(ys_ref, dest_ref[base + 1], buf.at[slot, 1], r, sems.at[slot]).start()
            return carry

        lax.fori_loop(0, t, start, 0, unroll=DMA_UNROLL)

    @pl.when(i == 0)
    def _():
        gather(0, 0)

    @pl.when(i + 1 < n_steps)
    def _():
        gather(i + 1, (i + 1) % 2)

    slot = i % 2

    def wait(r, carry):
        _row_copy(ys_ref, 0, buf.at[slot, 0], 0, sems.at[slot]).wait()
        _row_copy(ys_ref, 0, buf.at[slot, 1], 0, sems.at[slot]).wait()
        return carry

    lax.fori_loop(0, t, wait, 0, unroll=DMA_UNROLL)
    b = i // tiles_per_batch
    ew = ew_ref[...]
    ffn = buf[slot, 0] * ew[:, 0:1] + buf[slot, 1] * ew[:, 1:2]
    gt2 = mod_ref[pl.ds(b, 1), pl.ds(5 * D_MODEL, D_MODEL)]
    out_ref[...] = xl_ref[...] + gt2 * _rms(ffn, g_ref[...])


def _combine(dest_flat, ys, xl, ew, mod, g, tiles_per_batch):
    n, d = xl.shape
    t = TOK_TILE
    return pl.pallas_call(
        functools.partial(_combine_kernel, tiles_per_batch=tiles_per_batch),
        grid_spec=pltpu.PrefetchScalarGridSpec(
            num_scalar_prefetch=1,
            grid=(n // t,),
            in_specs=[pl.BlockSpec(memory_space=pl.ANY),
                      pl.BlockSpec((t, d), lambda i, dest: (i, 0)),
                      pl.BlockSpec((t, LANES), lambda i, dest: (i, 0)),
                      pl.BlockSpec(mod.shape, lambda i, dest: (0, 0)),
                      pl.BlockSpec(g.shape, lambda i, dest: (0, 0))],
            out_specs=pl.BlockSpec((t, d), lambda i, dest: (i, 0)),
            scratch_shapes=[pltpu.VMEM((2, 2, t, d), F32), pltpu.SemaphoreType.DMA((2,))]),
        out_shape=jax.ShapeDtypeStruct((n, d), F32),
        compiler_params=_cparams(1),
        name="moe_combine",
    )(dest_flat, ys, xl, ew, mod, g)


def kernel(x, c, ctx, c_ctx, w_ada, b_ada, g_pre_mix, g_post_mix, g_pre_ffn, g_post_ffn, w_in,
           conv_w, conv_b, dt_bias, a_log, d_skip, g_ssm_norm, hg_lb, g_hg_norm, w_branch_ssm,
           w_branch_hg, w_out, w_group_router, b_group_router, w_expert_router, b_expert_router,
           w_gate, w_up, w_down):
    nb, n_lat, d = x.shape
    n_ctx = ctx.shape[1]
    rows = n_lat // GRID_W
    assert w_ada.shape[0] == 1 and d == D_MODEL and nb + 1 <= SUBLANES
    assert rows % HG_CHUNK == 0 and n_ctx % (2 * HG_CHUNK) == 0

    w = w_in[0]
    o0 = 0
    w_z = w[:, o0:o0 + SSM_D_INNER]; o0 += SSM_D_INNER
    w_xbc = w[:, o0:o0 + SSM_CONV_DIM].astype(BF16); o0 += SSM_CONV_DIM
    w_dtr = w[:, o0:o0 + 2 * SSM_HEADS]; o0 += 2 * SSM_HEADS
    w_q = w[:, o0:o0 + HG_KEY].astype(BF16); o0 += HG_KEY
    w_f = jnp.stack([w[:, o0:o0 + HG_KEY], w[:, o0 + HG_KEY:o0 + 2 * HG_KEY]]).astype(BF16); o0 += 2 * HG_KEY
    w_i = w[:, o0:o0 + HG_KEY].astype(BF16); o0 += HG_KEY
    w_zg = jnp.concatenate([w_z, w[:, o0:o0 + 3 * d]], axis=1).astype(BF16)
    pad = jnp.zeros((d, LANES - SSM_HEADS), F32)
    w_dt = jnp.concatenate([w_dtr[:, :SSM_HEADS], pad, w_dtr[:, SSM_HEADS:], pad], axis=1).astype(BF16)
    zpad = jnp.zeros((LANES - SSM_HEADS,), F32)
    dtb = jnp.concatenate([dt_bias[0, 0], zpad, dt_bias[0, 1], zpad]).reshape(1, 2 * LANES)
    a_log_p = jnp.pad(a_log[0], ((0, 0), (0, LANES - SSM_HEADS))).reshape(2, 1, LANES)
    dskip_e = jnp.repeat(d_skip[0], SSM_HEAD_DIM, axis=-1).reshape(2, 1, SSM_D_INNER)
    head_of_lane = jnp.arange(SSM_D_INNER, dtype=I32) // SSM_HEAD_DIM
    expand = (jnp.arange(LANES, dtype=I32)[:, None] == head_of_lane[None, :]).astype(BF16)
    lb = jax.nn.softmax(hg_lb.astype(F32), axis=0)[0].reshape(2, 1, HG_KEY)
    w_r = jnp.concatenate([w_group_router[0],
                           jnp.transpose(w_expert_router[0], (1, 0, 2)).reshape(d, MOE_EXPERTS),
                           jnp.zeros((d, LANES - MOE_GROUPS - MOE_EXPERTS), F32)], axis=1)
    b_r = jnp.concatenate([b_group_router[0], b_expert_router[0].reshape(-1),
                           jnp.zeros((LANES - MOE_GROUPS - MOE_EXPERTS,), F32)]).reshape(1, LANES)
    g_hg = g_hg_norm[0].reshape(1, HG_DIM)

    cc = jnp.concatenate([c, c_ctx[None], jnp.zeros((SUBLANES - nb - 1, d), F32)], axis=0)
    mod = _modulation(cc, w_ada[0], b_ada[0])
    h_all, h_lat = _pre_norm(x, ctx, mod, g_pre_mix)

    xbc, dt = _ssd_inputs(h_all, w_xbc, w_dt, conv_w[0], conv_b, dtb)
    y = _ssd_scan(xbc, dt, a_log_p, dskip_e, expand)

    s0 = jnp.zeros((nb, 2, HG_HEADS, HG_DIM, HG_DIM), F32)
    _, s_ctx = _hg_scan(h_all[:, :n_ctx].astype(F32), 1, w_q, w_f, w_i, lb, s0)
    o_grid, _ = _hg_scan(h_lat.reshape(nb, rows, GRID_W, d), HG_COLS_PER_STEP, w_q, w_f, w_i, lb, s_ctx)
    o = o_grid.reshape(2, nb, n_lat, d)

    xl, h2, eid, ew = _merge(x, y, o, h_all, w_zg, w_branch_ssm[0].astype(BF16), w_branch_hg[0].astype(BF16),
                             w_out[0].astype(BF16), g_ssm_norm, g_hg, g_post_mix, g_pre_ffn, mod,
                             w_r, b_r)

    n = nb * n_lat
    cap = -(-(2 * n + MOE_EXPERTS * MOE_ROWS) // MOE_ROWS) * MOE_ROWS
    n_blocks = cap // MOE_ROWS
    dest, blk = _rank(eid.reshape(n, LANES), n_blocks)
    dest_flat = dest[:, 0:2].reshape(-1)
    xs = _dispatch(dest_flat, h2.reshape(n, d), cap)
    ys = _experts(blk[:n_blocks, 0], blk[0:1, 1], xs, w_gate[0], w_up[0], w_down[0])
    out = _combine(dest_flat, ys, xl.reshape(n, d), ew.reshape(n, LANES), mod, g_post_ffn,
                   n_lat // TOK_TILE)
    return out.reshape(nb, n_lat, d)
```

```python
import functools

import jax
import jax.numpy as jnp
from jax import lax
from jax.experimental import pallas as pl
from jax.experimental.pallas import tpu as pltpu

F32 = jnp.float32
BF16 = jnp.bfloat16
I32 = jnp.int32
HIGHEST = lax.Precision.HIGHEST

D_MODEL = 1024
GRID_W = 64
RMS_EPS = 1e-6
SSM_D_INNER = 2048
SSM_HEAD_DIM = 64
SSM_HEADS = 32
SSM_GROUPS = 4
SSM_HEADS_PER_GROUP = 8
SSM_STATE = 128
SSM_GROUP_WIDTH = SSM_HEADS_PER_GROUP * SSM_HEAD_DIM
SSM_CONV_DIM = SSM_D_INNER + 2 * SSM_GROUPS * SSM_STATE
HG_HEADS = 8
HG_DIM = 128
HG_KEY = HG_HEADS * HG_DIM
MOE_GROUPS = 4
MOE_EXPERTS_PER_GROUP = 8
MOE_EXPERTS = 32
MOE_D_FF = 512

LANES = 128
SUBLANES = 8
VMEM_LIMIT_BYTES = 56 * 1024 * 1024

TOK_TILE = 256
SSD_CHUNK = 128
HG_CHUNK = 128
HG_SUB = 16
HG_FACTOR_LOG2_RANGE = 110.0
HG_COLS_PER_STEP = 8
HG_COLS_PER_PASS = 4
MOE_ROWS = 256
RANK_TILE = 512
DMA_UNROLL = 8


def _cparams(n_axes):
    return pltpu.CompilerParams(dimension_semantics=("arbitrary",) * n_axes,
                                vmem_limit_bytes=VMEM_LIMIT_BYTES)


def _sigmoid(x):
    return 0.5 * jnp.tanh(0.5 * x) + 0.5


def _silu(x):
    return x * _sigmoid(x)


def _rms(v, g):
    ms = jnp.mean(v * v, axis=-1, keepdims=True)
    return v * lax.rsqrt(ms + RMS_EPS) * g


def _dot(a, b):
    return jnp.dot(a, b, preferred_element_type=F32)


def _dot_nt(a, b):
    return lax.dot_general(a, b, (((1,), (1,)), ((), ())), preferred_element_type=F32)


def _dot_tn(a, b):
    return lax.dot_general(a, b, (((0,), (0,)), ((), ())), preferred_element_type=F32)


def _mod_kernel(c_ref, w_ref, b_ref, o_ref):
    s = _silu(c_ref[...])
    o_ref[...] = jnp.dot(s, w_ref[...], precision=HIGHEST, preferred_element_type=F32) + b_ref[...]


def _modulation(cc, w_ada, b_ada):
    n = w_ada.shape[1]
    tn = 1024
    return pl.pallas_call(
        _mod_kernel,
        grid=(n // tn,),
        in_specs=[pl.BlockSpec((SUBLANES, D_MODEL), lambda j: (0, 0)),
                  pl.BlockSpec((D_MODEL, tn), lambda j: (0, j)),
                  pl.BlockSpec((1, tn), lambda j: (0, j))],
        out_specs=pl.BlockSpec((SUBLANES, tn), lambda j: (0, j)),
        out_shape=jax.ShapeDtypeStruct((SUBLANES, n), F32),
        compiler_params=_cparams(1),
        name="adaln_mod",
    )(cc, w_ada, b_ada.reshape(1, n))


def _norm_kernel(x_ref, ctx_ref, mod_ref, g_ref, all_ref, lat_ref, *, ctx_row):
    b = pl.program_id(0)
    i = pl.program_id(1)
    g = g_ref[...]

    def nm(v, row):
        sh = mod_ref[pl.ds(row, 1), pl.ds(0, D_MODEL)]
        sc = mod_ref[pl.ds(row, 1), pl.ds(D_MODEL, D_MODEL)]
        return _rms(v, g) * (1.0 + sc) + sh

    @pl.when(i == 0)
    def _():
        all_ref[...] = nm(ctx_ref[...], ctx_row).astype(BF16)

    @pl.when(i > 0)
    def _():
        h = nm(x_ref[...], b)
        all_ref[...] = h.astype(BF16)
        lat_ref[...] = h


def _pre_norm(x, ctx, mod, g):
    nb, n_lat, d = x.shape
    n_ctx = ctx.shape[1]
    assert n_ctx == TOK_TILE and n_lat % TOK_TILE == 0
    nt = n_lat // TOK_TILE
    return pl.pallas_call(
        functools.partial(_norm_kernel, ctx_row=nb),
        grid=(nb, nt + 1),
        in_specs=[pl.BlockSpec((None, TOK_TILE, d), lambda b, i: (b, jnp.maximum(i - 1, 0), 0)),
                  pl.BlockSpec((None, TOK_TILE, d), lambda b, i: (b, 0, 0)),
                  pl.BlockSpec(mod.shape, lambda b, i: (0, 0)),
                  pl.BlockSpec((1, d), lambda b, i: (0, 0))],
        out_specs=[pl.BlockSpec((None, TOK_TILE, d), lambda b, i: (b, i, 0)),
                   pl.BlockSpec((None, TOK_TILE, d), lambda b, i: (b, jnp.maximum(i - 1, 0), 0))],
        out_shape=[jax.ShapeDtypeStruct((nb, n_ctx + n_lat, d), BF16),
                   jax.ShapeDtypeStruct((nb, n_lat, d), F32)],
        compiler_params=_cparams(2),
        name="pre_norm",
    )(x, ctx, mod, g)


XBC_COL_TILE = 512


def _xbc_kernel(h_ref, hp_ref, hn_ref, w_ref, wdt_ref, cw_ref, cb_ref, dtb_ref, xbc_ref, dt_ref,
                raw_s, *, n_tiles):
    i = pl.program_id(1)
    h = h_ref[...]
    tm = h.shape[0]
    halo = hp_ref.shape[0]
    prev_ok = (i >= 2).astype(BF16)
    next_ok = jnp.logical_and(i >= 1, i < n_tiles - 1).astype(BF16)
    rows = jnp.concatenate([hp_ref[...] * prev_ok, h, hn_ref[...] * next_ok], axis=0)
    for c in range(SSM_CONV_DIM // XBC_COL_TILE):
        sl = slice(c * XBC_COL_TILE, (c + 1) * XBC_COL_TILE)
        raw_s[...] = _dot(rows, w_ref[:, sl])
        up = raw_s[halo - 1:halo - 1 + tm, :]
        mid = raw_s[halo:halo + tm, :]
        dn = raw_s[halo + 1:halo + 1 + tm, :]
        conv = cw_ref[0:1, sl] * up + cw_ref[1:2, sl] * mid + cw_ref[2:3, sl] * dn + cb_ref[:, sl]
        xbc_ref[:, sl] = _silu(conv).astype(BF16)
    u = _dot(h, wdt_ref[...]) + dtb_ref[...]
    dt_ref[...] = jnp.maximum(u, 0.0) + jnp.log(1.0 + jnp.exp(-jnp.abs(u)))


def _ssd_inputs(h_all, w_xbc, w_dt, conv_w, conv_b, dt_bias):
    nb, ltot, d = h_all.shape
    nt = ltot // TOK_TILE
    halo = 2 * SUBLANES
    n_halo = ltot // halo
    per = TOK_TILE // halo
    return pl.pallas_call(
        functools.partial(_xbc_kernel, n_tiles=nt),
        grid=(nb, nt),
        in_specs=[pl.BlockSpec((None, TOK_TILE, d), lambda b, i: (b, i, 0)),
                  pl.BlockSpec((None, halo, d), lambda b, i: (b, jnp.maximum(i * per - 1, 0), 0)),
                  pl.BlockSpec((None, halo, d), lambda b, i: (b, jnp.minimum((i + 1) * per, n_halo - 1), 0)),
                  pl.BlockSpec(w_xbc.shape, lambda b, i: (0, 0)),
                  pl.BlockSpec(w_dt.shape, lambda b, i: (0, 0)),
                  pl.BlockSpec(conv_w.shape, lambda b, i: (0, 0)),
                  pl.BlockSpec(conv_b.shape, lambda b, i: (0, 0)),
                  pl.BlockSpec(dt_bias.shape, lambda b, i: (0, 0))],
        out_specs=[pl.BlockSpec((None, TOK_TILE, SSM_CONV_DIM), lambda b, i: (b, i, 0)),
                   pl.BlockSpec((None, TOK_TILE, 2 * LANES), lambda b, i: (b, i, 0))],
        out_shape=[jax.ShapeDtypeStruct((nb, ltot, SSM_CONV_DIM), BF16),
                   jax.ShapeDtypeStruct((nb, ltot, 2 * LANES), F32)],
        scratch_shapes=[pltpu.VMEM((TOK_TILE + 2 * halo, XBC_COL_TILE), F32)],
        compiler_params=_cparams(2),
        name="ssd_inputs",
    )(h_all, h_all, h_all, w_xbc, w_dt, conv_w, conv_b, dt_bias)


def _ssd_kernel(xbc_ref, dt_ref, alog_ref, dskip_ref, e_ref, y_ref, state):
    d = pl.program_id(1)
    step = pl.program_id(2)
    q = SSD_CHUNK

    @pl.when(step == 0)
    def _():
        state[...] = jnp.zeros(state.shape, F32)

    lane = lax.broadcasted_iota(I32, (1, LANES), 1)
    a = jnp.where(lane < SSM_HEADS, -jnp.exp(alog_ref[...]), 0.0)
    dt = dt_ref[...]
    adt = dt * a
    r = lax.broadcasted_iota(I32, (q, q), 0)
    c = lax.broadcasted_iota(I32, (q, q), 1)
    mask = jnp.where(d == 0, r - c, c - r) >= 0
    cs = jnp.dot(mask.astype(F32), adt, precision=HIGHEST, preferred_element_type=F32)
    tot = jnp.sum(adt, axis=0, keepdims=True)
    cs_t = cs.T
    lhs = jnp.concatenate([dt, jnp.exp(cs), jnp.exp(tot - cs),
                           jnp.broadcast_to(jnp.exp(tot), (SUBLANES, LANES))], axis=0).astype(BF16)
    lane_h = lax.broadcasted_iota(I32, (q, LANES), 1)
    for g in range(SSM_GROUPS):
        gs = slice(g * SSM_GROUP_WIDTH, (g + 1) * SSM_GROUP_WIDTH)
        ex = _dot(lhs, e_ref[:, gs])
        dt_e, e1_e, e2_e, etot_e = ex[0:q], ex[q:2 * q], ex[2 * q:3 * q], ex[3 * q:3 * q + 1]
        x = xbc_ref[:, gs].astype(F32)
        bg = xbc_ref[:, SSM_D_INNER + g * SSM_STATE:SSM_D_INNER + (g + 1) * SSM_STATE]
        cg = xbc_ref[:, SSM_D_INNER + (SSM_GROUPS + g) * SSM_STATE:
                     SSM_D_INNER + (SSM_GROUPS + g + 1) * SSM_STATE]
        xdt = x * dt_e
        cb = _dot_nt(cg, bg)
        s_in = state[g]
        y_off = _dot(cg, s_in.astype(BF16)) * e1_e
        state[g] = s_in * etot_e + _dot_tn(bg, (xdt * e2_e).astype(BF16))
        parts = []
        for m in range(SSM_HEADS_PER_GROUP // 2):
            xp = xdt[:, m * LANES:(m + 1) * LANES]
            acc = None
            for half in range(2):
                hd = g * SSM_HEADS_PER_GROUP + 2 * m + half
                col = cs[:, hd:hd + 1]
                rowv = cs_t[hd:hd + 1, :]
                decay = jnp.exp(jnp.minimum(col - rowv, 0.0))
                w = jnp.where(mask, cb * decay, 0.0).astype(BF16)
                sel = (lane_h < SSM_HEAD_DIM) if half == 0 else (lane_h >= SSM_HEAD_DIM)
                t = _dot(w, jnp.where(sel, xp, 0.0).astype(BF16))
                acc = t if acc is None else acc + t
            parts.append(acc)
        y = jnp.concatenate(parts, axis=1) + y_off + x * dskip_ref[:, gs]
        y_ref[:, gs] = y.astype(BF16)


def _ssd_scan(xbc, dt, a_log_p, dskip_e, expand):
    nb, ltot, _ = xbc.shape
    q = SSD_CHUNK
    nc = ltot // q
    n_ctx_chunks = TOK_TILE // q

    def chunk(d, s):
        rev = jnp.where(s < n_ctx_chunks, n_ctx_chunks - 1 - s, nc + n_ctx_chunks - 1 - s)
        return jnp.where(d == 0, s, rev)

    return pl.pallas_call(
        _ssd_kernel,
        grid=(nb, 2, nc),
        in_specs=[pl.BlockSpec((None, q, SSM_CONV_DIM), lambda b, d, s: (b, chunk(d, s), 0)),
                  pl.BlockSpec((None, q, LANES), lambda b, d, s: (b, chunk(d, s), d)),
                  pl.BlockSpec((None, 1, LANES), lambda b, d, s: (d, 0, 0)),
                  pl.BlockSpec((None, 1, SSM_D_INNER), lambda b, d, s: (d, 0, 0)),
                  pl.BlockSpec(expand.shape, lambda b, d, s: (0, 0))],
        out_specs=pl.BlockSpec((None, None, q, SSM_D_INNER), lambda b, d, s: (d, b, chunk(d, s), 0)),
        out_shape=jax.ShapeDtypeStruct((2, nb, ltot, SSM_D_INNER), BF16),
        scratch_shapes=[pltpu.VMEM((SSM_GROUPS, SSM_STATE, SSM_GROUP_WIDTH), F32)],
        compiler_params=_cparams(3),
        name="ssd_scan",
    )(xbc, dt, a_log_p, dskip_e, expand)


def _hg_intra(q_s, k_s, b_s, r0, off):
    c = HG_CHUNK
    cols = pl.ds(off, HG_DIM)
    lane8 = lax.broadcasted_iota(I32, (SUBLANES, c), 1)
    lane_sub = lax.broadcasted_iota(I32, (HG_SUB, c), 1)
    rows = []
    for i in range(c // HG_SUB):
        lo = r0 + i * HG_SUB
        bi = b_s[lo:lo + HG_SUB, cols]
        qi = q_s[lo:lo + HG_SUB, cols]
        top = jnp.zeros((SUBLANES, c), F32)
        bot = jnp.zeros((SUBLANES, c), F32)
        for j in range(HG_SUB):
            s = lo + j
            at = i * HG_SUB + j
            brow = b_s[s:s + 1, cols]
            krow = k_s[s:s + 1, cols]
            if j < SUBLANES:
                p = qi * krow * jnp.exp2(bi - brow)
                top = jnp.where(lane8 == at, jnp.sum(p[0:SUBLANES], axis=1, keepdims=True), top)
                bot = jnp.where(lane8 == at, jnp.sum(p[SUBLANES:], axis=1, keepdims=True), bot)
            else:
                p = qi[SUBLANES:] * krow * jnp.exp2(bi[SUBLANES:] - brow)
                bot = jnp.where(lane8 == at, jnp.sum(p, axis=1, keepdims=True), bot)
        a_row = jnp.concatenate([top, bot], axis=0)
        if i > 0:
            bref = b_s[lo - 1:lo, cols]
            qt = qi * jnp.exp2(bi - bref)
            kt = k_s[r0:lo, cols] * jnp.exp2(bref - b_s[r0:lo, cols])
            ktp = jnp.concatenate([kt, jnp.zeros((c - i * HG_SUB, HG_DIM), F32)], axis=0)
            a_off = _dot_nt(qt.astype(BF16), ktp.astype(BF16))
            a_row = jnp.where(lane_sub < i * HG_SUB, a_off, a_row)
        rows.append(a_row)
    a = jnp.concatenate(rows, axis=0)
    r64 = lax.broadcasted_iota(I32, (c, c), 0)
    c64 = lax.broadcasted_iota(I32, (c, c), 1)
    return jnp.where(r64 >= c64, a, 0.0)


def _hg_kernel(h_ref, wq_ref, wf_ref, wi_ref, lb_ref, sin_ref, o_ref, sout_ref,
               pin_s, pout_s, q_s, k_s, b_s, v_s, qe_s, kd_s, ki_s, dec_s, o_s, state, *, log2_rows,
               log2_ncol, pass_rows):
    d = pl.program_id(1)
    step = pl.program_id(2)
    nsteps = pl.num_programs(2)
    t_all = 1 << (log2_rows + log2_ncol)
    c = HG_CHUNK
    nch = pass_rows // c

    def source_row(p):
        src = ((p & ((1 << log2_rows) - 1)) << log2_ncol) + (p >> log2_rows)
        return jnp.where(d == 0, src, t_all - 1 - src)

    @pl.when(step == 0)
    def _():
        state[...] = sin_ref[...]
        shape = (t_all, t_all)
        pin_s[...] = (lax.broadcasted_iota(I32, shape, 1)
                      == source_row(lax.broadcasted_iota(I32, shape, 0))).astype(BF16)
        pout_s[...] = (lax.broadcasted_iota(I32, shape, 0)
                       == source_row(lax.broadcasted_iota(I32, shape, 1))).astype(BF16)

    r64 = lax.broadcasted_iota(I32, (c, c), 0)
    c64 = lax.broadcasted_iota(I32, (c, c), 1)
    tril = (r64 >= c64).astype(BF16)
    h_rows =h_ref[...].reshape(t_all, HG_KEY).astype(BF16)

    for ps in range(t_all // pass_rows):
        prs = slice(ps * pass_rows, (ps + 1) * pass_rows)
        hb = _dot(pin_s[prs, :], h_rows).astype(BF16)
        q_s[...] = _silu(_dot(hb, wq_ref[...]))
        fz = _dot(hb, wf_ref[...])
        t = jnp.exp(-jnp.abs(fz))
        r = 1.0 / (1.0 + t)
        big, small = r, t * r
        pos = fz >= 0.0
        lb = lb_ref[...]
        lf2 = jnp.log2(lb + (1.0 - lb) * jnp.where(pos, big, small))
        k_s[...] = (1.0 - lb) * jnp.where(pos, small, big)
        v_s[...] = _dot(hb, wi_ref[...]).astype(BF16)
        for ch in range(nch):
            rs = slice(ch * c, (ch + 1) * c)
            x = lf2[rs]
            hi = x.astype(BF16)
            rem = x - hi.astype(F32)
            mid = rem.astype(BF16)
            low = (rem - mid.astype(F32)).astype(BF16)
            b = _dot(tril, hi) + _dot(tril, mid) + _dot(tril, low)
            b_s[rs, :] = b
            btot = b[c - 1:c]
            qe_s[rs, :] = (q_s[rs, :] * jnp.exp2(b)).astype(BF16)
            kd_s[rs, :] = (k_s[rs, :] * jnp.exp2(btot - b)).astype(BF16)
            ki_s[rs, :] = (k_s[rs, :] * jnp.exp2(-b)).astype(BF16)
            dec_s[ch:ch + 1, :] = jnp.exp2(btot)
            steepest = -btot if ch == 0 else jnp.maximum(steepest, -btot)
        bounded = jnp.max(steepest) <= HG_FACTOR_LOG2_RANGE

        def head(hh, carry, factored):
            off = pl.multiple_of(hh * HG_DIM, HG_DIM)
            cols = pl.ds(off, HG_DIM)
            s_t = state[hh]
            for ch in range(nch):
                rs = slice(ch * c, (ch + 1) * c)
                if factored:
                    a = jnp.where(r64 >= c64, _dot_nt(qe_s[rs, cols], ki_s[rs, cols]), 0.0)
                else:
                    a = _hg_intra(q_s, k_s, b_s, ch * c, off)
                v = v_s[rs, cols]
                o_s[rs, cols] = _dot_nt(qe_s[rs, cols], s_t.astype(BF16)) + _dot(a.astype(BF16), v)
                s_t = s_t * dec_s[ch:ch + 1, cols] + _dot_tn(v, kd_s[rs, cols])
            state[hh] = s_t
            return carry

        @pl.when(bounded)
        def _():
            lax.fori_loop(0, HG_HEADS, functools.partial(head, factored=True), 0)

        @pl.when(jnp.logical_not(bounded))
        def _():
            lax.fori_loop(0, HG_HEADS, functools.partial(head, factored=False), 0)
        back = _dot(pout_s[:, prs], o_s[...].astype(BF16)).reshape(o_ref.shape)
        if ps == 0:
            o_ref[...] = back
        else:
            o_ref[...] += back

    @pl.when(step == nsteps - 1)
    def _():
        sout_ref[...] = state[...]


def _hg_scan(h_grid, ncol, wq, wf, wi, lb, s_in):
    if h_grid.ndim == 3:
        nb, rows, d_model = h_grid.shape
        ncol_total = 1
    else:
        nb, rows, ncol_total, d_model = h_grid.shape
    nsteps = ncol_total // ncol
    t_all = rows * ncol
    pass_rows = min(t_all, HG_COLS_PER_PASS * LANES)
    log2_rows, log2_ncol = rows.bit_length() - 1, ncol.bit_length() - 1
    assert rows == 1 << log2_rows and ncol == 1 << log2_ncol and t_all % pass_rows == 0
    assert pass_rows // HG_CHUNK <= SUBLANES

    def pos(d, s):
        return jnp.where(d == 0, s, nsteps - 1 - s)

    once = pl.Buffered(1)
    if h_grid.ndim == 3:
        h_spec = pl.BlockSpec((None, rows, d_model), lambda b, d, s: (b, 0, 0))
        o_spec = pl.BlockSpec((None, None, rows, d_model), lambda b, d, s: (d, b, 0, 0))
    else:
        h_spec = pl.BlockSpec((None, rows, ncol, d_model), lambda b, d, s: (b, 0, pos(d, s), 0))
        o_spec = pl.BlockSpec((None, None, rows, ncol, d_model), lambda b, d, s: (d, b, 0, pos(d, s), 0))
    st_spec = pl.BlockSpec((None, None, HG_HEADS, HG_DIM, HG_DIM), lambda b, d, s: (b, d, 0, 0, 0))
    wide_f32 = pltpu.VMEM((pass_rows, HG_KEY), F32)
    wide_bf16 = pltpu.VMEM((pass_rows, HG_KEY), BF16)
    perm = pltpu.VMEM((t_all, t_all), BF16)
    return pl.pallas_call(
        functools.partial(_hg_kernel, log2_rows=log2_rows, log2_ncol=log2_ncol, pass_rows=pass_rows),
        grid=(nb, 2, nsteps),
        in_specs=[h_spec,
                  pl.BlockSpec(wq.shape, lambda b, d, s: (0, 0), pipeline_mode=once),
                  pl.BlockSpec((None, d_model, HG_KEY), lambda b, d, s: (d, 0, 0), pipeline_mode=once),
                  pl.BlockSpec(wi.shape, lambda b, d, s: (0, 0), pipeline_mode=once),
                  pl.BlockSpec((None, 1, HG_KEY), lambda b, d, s: (d, 0, 0)),
                  st_spec],
        out_specs=[o_spec, st_spec],
        out_shape=[jax.ShapeDtypeStruct((2,) + h_grid.shape, F32),
                   jax.ShapeDtypeStruct(s_in.shape, F32)],
        scratch_shapes=[perm, perm, wide_f32, wide_f32, wide_f32, wide_bf16, wide_bf16, wide_bf16, wide_bf16,
                        pltpu.VMEM((SUBLANES, HG_KEY), F32), wide_f32,
                        pltpu.VMEM((HG_HEADS, HG_DIM, HG_DIM), F32)],
        compiler_params=_cparams(3),
        name="hgrn2_scan",
    )(h_grid, wq, wf, wi, lb, s_in)


def _merge_kernel(x_ref, yf_ref, yb_ref, of_ref, ob_ref, h_ref, wzg_ref, wbs_ref, wbh_ref, wo_ref,
                  gs_ref, gh_ref, gpm_ref, gpf_ref, mod_ref, wr_ref, br_ref,
                  xl_ref, h2_ref, eid_ref, ew_ref):
    b = pl.program_id(0)
    tm = x_ref.shape[0]
    d = D_MODEL
    h = h_ref[...]
    z = _dot(h, wzg_ref[:, 0:SSM_D_INNER])
    yz = (yf_ref[...].astype(F32) + yb_ref[...].astype(F32)) * _silu(z)
    parts = []
    for g in range(SSM_GROUPS):
        gsl = slice(g * SSM_GROUP_WIDTH, (g + 1) * SSM_GROUP_WIDTH)
        parts.append(_rms(yz[:, gsl], gs_ref[:, gsl]).astype(BF16))
    br_ssm = _dot(jnp.concatenate(parts, axis=1), wbs_ref[...])

    o = of_ref[...].astype(F32) + ob_ref[...].astype(F32)
    og = _dot(h, wzg_ref[:, SSM_D_INNER:SSM_D_INNER + d])
    parts = []
    for hh in range(HG_HEADS):
        hsl = slice(hh * HG_DIM, (hh + 1) * HG_DIM)
        parts.append((_rms(o[:, hsl], gh_ref[...]) * _silu(og[:, hsl])).astype(BF16))
    br_hg = _dot(jnp.concatenate(parts, axis=1), wbh_ref[...])

    gm = _dot(h, wzg_ref[:, SSM_D_INNER + d:SSM_D_INNER + 2 * d])
    gh = _dot(h, wzg_ref[:, SSM_D_INNER + 2 * d:SSM_D_INNER + 3 * d])
    mixed = _sigmoid(gm) * br_ssm + _sigmoid(gh) * br_hg
    mix = _dot(mixed.astype(BF16), wo_ref[...])

    gt1 = mod_ref[pl.ds(b, 1), pl.ds(2 * d, d)]
    sh2 = mod_ref[pl.ds(b, 1), pl.ds(3 * d, d)]
    sc2 = mod_ref[pl.ds(b, 1), pl.ds(4 * d, d)]
    xl = x_ref[...] + gt1 * _rms(mix, gpm_ref[...])
    xl_ref[...] = xl
    h2 = _rms(xl, gpf_ref[...]) * (1.0 + sc2) + sh2
    h2_ref[...] = h2

    logits = jnp.dot(h2, wr_ref[...], precision=HIGHEST, preferred_element_type=F32) + br_ref[...]
    lane = lax.broadcasted_iota(I32, (tm, LANES), 1)
    lane_f = lane.astype(F32)
    neg = jnp.float32(-jnp.inf)
    far = jnp.float32(2 * LANES)
    gl = jnp.where(lane < MOE_GROUPS, logits, neg)
    gmax = jnp.max(gl, axis=1, keepdims=True)
    p_grp = 1.0 / jnp.sum(jnp.exp(gl - gmax), axis=1, keepdims=True)
    grp = jnp.min(jnp.where(gl == gmax, lane_f, far), axis=1, keepdims=True)
    lo = MOE_GROUPS + MOE_EXPERTS_PER_GROUP * grp
    el = jnp.where(jnp.logical_and(lane_f >= lo, lane_f < lo + MOE_EXPERTS_PER_GROUP), logits, neg)
    m1 = jnp.max(el, axis=1, keepdims=True)
    i1 = jnp.min(jnp.where(el == m1, lane_f, far), axis=1, keepdims=True)
    el2 = jnp.where(lane_f == i1, neg, el)
    m2 = jnp.max(el2, axis=1, keepdims=True)
    i2 = jnp.min(jnp.where(el2 == m2, lane_f, far), axis=1, keepdims=True)
    e2 = jnp.exp(m2 - m1)
    w1 = p_grp / (1.0 + e2)
    w2 = p_grp * e2 / (1.0 + e2)
    eid = jnp.where(lane == 0, i1, i2) - float(MOE_GROUPS)
    eid_ref[...] = jnp.where(lane < 2, eid, 0.0).astype(I32)
    ew_ref[...] = jnp.where(lane == 0, w1, jnp.where(lane == 1, w2, 0.0))


def _merge(x, y, o, h_all, w_zg, w_bs, w_bh, w_o, g_ssm, g_hg, g_pm, g_pf, mod, w_r, b_r):
    nb, n_lat, d = x.shape
    tm = TOK_TILE
    ctx_tiles = (y.shape[2] - n_lat) // tm

    def full(a):
        return pl.BlockSpec(a.shape, lambda b, i: (0,) * a.ndim, pipeline_mode=pl.Buffered(1))

    tok = lambda w: pl.BlockSpec((None, tm, w), lambda b, i: (b, i, 0))
    return pl.pallas_call(
        _merge_kernel,
        grid=(nb, n_lat // tm),
        in_specs=[tok(d),
                  pl.BlockSpec((None, None, tm, SSM_D_INNER), lambda b, i: (0, b, i + ctx_tiles, 0)),
                  pl.BlockSpec((None, None, tm, SSM_D_INNER), lambda b, i: (1, b, i + ctx_tiles, 0)),
                  pl.BlockSpec((None, None, tm, d), lambda b, i: (0, b, i, 0)),
                  pl.BlockSpec((None, None, tm, d), lambda b, i: (1, b, i, 0)),
                  pl.BlockSpec((None, tm, d), lambda b, i: (b, i + ctx_tiles, 0)),
                  full(w_zg), full(w_bs), full(w_bh), full(w_o), full(g_ssm), full(g_hg), full(g_pm),
                  full(g_pf), full(mod), full(w_r), full(b_r)],
        out_specs=[tok(d), tok(d), tok(LANES), tok(LANES)],
        out_shape=[jax.ShapeDtypeStruct((nb, n_lat, d), F32),
                   jax.ShapeDtypeStruct((nb, n_lat, d), F32),
                   jax.ShapeDtypeStruct((nb, n_lat, LANES), I32),
                   jax.ShapeDtypeStruct((nb, n_lat, LANES), F32)],
        compiler_params=_cparams(2),
        name="merge_route",
    )(x, y, y, o, o, h_all, w_zg, w_bs, w_bh, w_o, g_ssm, g_hg, g_pm, g_pf, mod, w_r, b_r)


def _rank_kernel(eid_ref, dest_ref, blk_ref, cnt, base):
    p = pl.program_id(0)
    i = pl.program_id(1)
    t = eid_ref.shape[0]
    lane = lax.broadcasted_iota(I32, (t, LANES), 1)
    e = eid_ref[...]
    oh0 = (lane == e[:, 0:1]).astype(F32)
    oh1 = (lane == e[:, 1:2]).astype(F32)
    c0 = jnp.sum(oh0, axis=0, keepdims=True)
    c1 = jnp.sum(oh1, axis=0, keepdims=True)

    @pl.when(jnp.logical_and(p == 0, i == 0))
    def _():
        cnt[...] = jnp.zeros(cnt.shape, F32)

    @pl.when(p == 0)
    def _():
        cnt[...] = cnt[...] + c0 + c1

    @pl.when(jnp.logical_and(p == 1, i == 0))
    def _():
        lane1 = lax.broadcasted_iota(I32, (SUBLANES, LANES), 1)
        padded = jnp.floor((cnt[...] + (MOE_ROWS - 1)) / MOE_ROWS) * MOE_ROWS
        padded = jnp.broadcast_to(jnp.where(lane1[0:1] < MOE_EXPERTS, padded, 0.0), (SUBLANES, LANES))
        rr = lax.broadcasted_iota(I32, (LANES, LANES), 0)
        cc = lax.broadcasted_iota(I32, (LANES, LANES), 1)
        start = jnp.dot(padded, (rr < cc).astype(F32), precision=HIGHEST, preferred_element_type=F32)
        base[...] = start[0:1]
        pad_end = start[0:1] + padded[0:1]
        pad_end = jnp.where(lane1[0:1] < MOE_EXPERTS, pad_end, jnp.float32(2.0 ** 30))
        nblk = blk_ref.shape[0]
        bstart = lax.broadcasted_iota(I32, (nblk, LANES), 0).astype(F32) * MOE_ROWS
        owner = jnp.sum((pad_end <= bstart).astype(F32), axis=1, keepdims=True)
        owner = jnp.minimum(owner, MOE_EXPERTS - 1)
        total = jnp.max(jnp.where(lane1[0:1] == MOE_EXPERTS - 1, pad_end, 0.0), axis=1, keepdims=True)
        lane_b = lax.broadcasted_iota(I32, (nblk, LANES), 1)
        blk_ref[...] = jnp.where(lane_b == 0, owner, total / MOE_ROWS).astype(I32)

    @pl.when(p == 1)
    def _():
        rr = lax.broadcasted_iota(I32, (t, t), 0)
        cc = lax.broadcasted_iota(I32, (t, t), 1)
        before = (rr > cc).astype(BF16)
        b0 = base[...]
        r0 = _dot(before, oh0.astype(BF16)) + b0
        r1 = _dot(before, oh1.astype(BF16)) + b0 + c0
        d0 = jnp.sum(oh0 * r0, axis=1, keepdims=True)
        d1 = jnp.sum(oh1 * r1, axis=1, keepdims=True)
        dest_ref[...] = jnp.where(lane == 0, d0, jnp.where(lane == 1, d1, 0.0)).astype(I32)
        base[...] = b0 + c0 + c1


def _rank(eid, n_blocks):
    n = eid.shape[0]
    t = RANK_TILE
    nblk_pad = -(-n_blocks // SUBLANES) * SUBLANES
    return pl.pallas_call(
        _rank_kernel,
        grid=(2, n // t),
        in_specs=[pl.BlockSpec((t, LANES), lambda p, i: (i, 0))],
        out_specs=[pl.BlockSpec((t, LANES), lambda p, i: (p * i, 0)),
                   pl.BlockSpec((nblk_pad, LANES), lambda p, i: (0, 0))],
        out_shape=[jax.ShapeDtypeStruct((n, LANES), I32),
                   jax.ShapeDtypeStruct((nblk_pad, LANES), I32)],
        scratch_shapes=[pltpu.VMEM((1, LANES), F32), pltpu.VMEM((1, LANES), F32)],
        compiler_params=_cparams(2),
        name="moe_rank",
    )(eid)


def _row_copy(src, s, dst, t, sem):
    return pltpu.make_async_copy(src.at[pl.ds(s, 1)], dst.at[pl.ds(t, 1)], sem)


def _dispatch_kernel(dest_ref, h_ref, zin_ref, out_ref, sem):
    del zin_ref
    i = pl.program_id(0)
    t = h_ref.shape[0]

    def start(r, carry):
        base = (i * t + r) * 2
        _row_copy(h_ref, r, out_ref, dest_ref[base], sem).start()
        _row_copy(h_ref, r, out_ref, dest_ref[base + 1], sem).start()
        return carry

    lax.fori_loop(0, t, start, 0, unroll=DMA_UNROLL)

    def wait(r, carry):
        _row_copy(h_ref, 0, out_ref, 0, sem).wait()
        _row_copy(h_ref, 0, out_ref, 0, sem).wait()
        return carry

    lax.fori_loop(0, t, wait, 0, unroll=DMA_UNROLL)


def _dispatch(dest_flat, h2, cap):
    n, d = h2.shape
    t = TOK_TILE
    return pl.pallas_call(
        _dispatch_kernel,
        grid_spec=pltpu.PrefetchScalarGridSpec(
            num_scalar_prefetch=1,
            grid=(n // t,),
            in_specs=[pl.BlockSpec((t, d), lambda i, dest: (i, 0)),
                      pl.BlockSpec(memory_space=pl.ANY)],
            out_specs=pl.BlockSpec(memory_space=pl.ANY),
            scratch_shapes=[pltpu.SemaphoreType.DMA]),
        out_shape=jax.ShapeDtypeStruct((cap, d), F32),
        input_output_aliases={2: 0},
        compiler_params=_cparams(1),
        name="moe_dispatch",
    )(dest_flat, h2, jnp.zeros((cap, d), F32))


def _expert_kernel(be_ref, na_ref, x_ref, wg_ref, wu_ref, wd_ref, y_ref, wg_s, wu_s, wd_s):
    i = pl.program_id(0)
    e = be_ref[i]
    prev = be_ref[jnp.maximum(i - 1, 0)]

    @pl.when(jnp.logical_or(i == 0, e != prev))
    def _():
        wg_s[...] = wg_ref[...].astype(BF16)
        wu_s[...] = wu_ref[...].astype(BF16)
        wd_s[...] = wd_ref[...].astype(BF16)

    @pl.when(i < na_ref[0])
    def _():
        x = x_ref[...].astype(BF16)
        hid = _silu(_dot(x, wg_s[...])) * _dot(x, wu_s[...])
        y_ref[...] = _dot(hid.astype(BF16), wd_s[...])

    @pl.when(i >= na_ref[0])
    def _():
        y_ref[...] = jnp.zeros(y_ref.shape, F32)


def _experts(blk_e, n_act, xs, w_gate, w_up, w_down):
    cap, d = xs.shape
    r = MOE_ROWS
    ff = w_gate.shape[2]
    return pl.pallas_call(
        _expert_kernel,
        grid_spec=pltpu.PrefetchScalarGridSpec(
            num_scalar_prefetch=2,
            grid=(cap // r,),
            in_specs=[pl.BlockSpec((r, d), lambda i, be, na: (i, 0)),
                      pl.BlockSpec((None, d, ff), lambda i, be, na: (be[i], 0, 0)),
                      pl.BlockSpec((None, d, ff), lambda i, be, na: (be[i], 0, 0)),
                      pl.BlockSpec((None, ff, d), lambda i, be, na: (be[i], 0, 0))],
            out_specs=pl.BlockSpec((r, d), lambda i, be, na: (i, 0)),
            scratch_shapes=[pltpu.VMEM((d, ff), BF16), pltpu.VMEM((d, ff), BF16),
                            pltpu.VMEM((ff, d), BF16)]),
        out_shape=jax.ShapeDtypeStruct((cap, d), F32),
        compiler_params=_cparams(1),
        name="moe_experts",
    )(blk_e, n_act, xs, w_gate, w_up, w_down)


def _combine_kernel(dest_ref, ys_ref, xl_ref, ew_ref, mod_ref, g_ref, out_ref, buf, sems, *, tiles_per_batch):
    i = pl.program_id(0)
    n_steps = pl.num_programs(0)
    t = xl_ref.shape[0]

    def gather(tile, slot):
        def start(r, carry):
            base = (tile * t + r) * 2
            _row_copy(ys_ref, dest_ref[base], buf.at[slot, 0], r, sems.at[slot]).start()
            _row_copy(ys_ref, dest_ref[base + 1], buf.at[slot, 1], r, sems.at[slot]).start()
            return carry

        lax.fori_loop(0, t, start, 0, unroll=DMA_UNROLL)

    @pl.when(i == 0)
    def _():
        gather(0, 0)

    @pl.when(i + 1 < n_steps)
    def _():
        gather(i + 1, (i + 1) % 2)

    slot = i % 2

    def wait(r, carry):
        _row_copy(ys_ref, 0, buf.at[slot, 0], 0, sems.at[slot]).wait()
        _row_copy(ys_ref, 0, buf.at[slot, 1], 0, sems.at[slot]).wait()
        return carry

    lax.fori_loop(0, t, wait, 0, unroll=DMA_UNROLL)
    b = i // tiles_per_batch
    ew = ew_ref[...]
    ffn = buf[slot, 0] * ew[:, 0:1] + buf[slot, 1] * ew[:, 1:2]
    gt2 = mod_ref[pl.ds(b, 1), pl.ds(5 * D_MODEL, D_MODEL)]
    out_ref[...] = xl_ref[...] + gt2 * _rms(ffn, g_ref[...])


def _combine(dest_flat, ys, xl, ew, mod, g, tiles_per_batch):
    n, d = xl.shape
    t = TOK_TILE
    return pl.pallas_call(
        functools.partial(_combine_kernel, tiles_per_batch=tiles_per_batch),
        grid_spec=pltpu.PrefetchScalarGridSpec(
            num_scalar_prefetch=1,
            grid=(n // t,),
            in_specs=[pl.BlockSpec(memory_space=pl.ANY),
                      pl.BlockSpec((t, d), lambda i, dest: (i, 0)),
                      pl.BlockSpec((t, LANES), lambda i, dest: (i, 0)),
                      pl.BlockSpec(mod.shape, lambda i, dest: (0, 0)),
                      pl.BlockSpec(g.shape, lambda i, dest: (0, 0))],
            out_specs=pl.BlockSpec((t, d), lambda i, dest: (i, 0)),
            scratch_shapes=[pltpu.VMEM((2, 2, t, d), F32), pltpu.SemaphoreType.DMA((2,))]),
        out_shape=jax.ShapeDtypeStruct((n, d), F32),
        compiler_params=_cparams(1),
        name="moe_combine",
    )(dest_flat, ys, xl, ew, mod, g)


def kernel(x, c, ctx, c_ctx, w_ada, b_ada, g_pre_mix, g_post_mix, g_pre_ffn, g_post_ffn, w_in,
           conv_w, conv_b, dt_bias, a_log, d_skip, g_ssm_norm, hg_lb, g_hg_norm, w_branch_ssm,
           w_branch_hg, w_out, w_group_router, b_group_router, w_expert_router, b_expert_router,
           w_gate, w_up, w_down):
    nb, n_lat, d = x.shape
    n_ctx = ctx.shape[1]
    rows = n_lat // GRID_W
    assert w_ada.shape[0] == 1 and d == D_MODEL and nb + 1 <= SUBLANES
    assert rows % HG_CHUNK == 0 and n_ctx % (2 * HG_CHUNK) == 0

    w = w_in[0]
    o0 = 0
    w_z = w[:, o0:o0 + SSM_D_INNER]; o0 += SSM_D_INNER
    w_xbc = w[:, o0:o0 + SSM_CONV_DIM].astype(BF16); o0 += SSM_CONV_DIM
    w_dtr = w[:, o0:o0 + 2 * SSM_HEADS]; o0 += 2 * SSM_HEADS
    w_q = w[:, o0:o0 + HG_KEY].astype(BF16); o0 += HG_KEY
    w_f = jnp.stack([w[:, o0:o0 + HG_KEY], w[:, o0 + HG_KEY:o0 + 2 * HG_KEY]]).astype(BF16); o0 += 2 * HG_KEY
    w_i = w[:, o0:o0 + HG_KEY].astype(BF16); o0 += HG_KEY
    w_zg = jnp.concatenate([w_z, w[:, o0:o0 + 3 * d]], axis=1).astype(BF16)
    pad = jnp.zeros((d, LANES - SSM_HEADS), F32)
    w_dt = jnp.concatenate([w_dtr[:, :SSM_HEADS], pad, w_dtr[:, SSM_HEADS:], pad], axis=1).astype(BF16)
    zpad = jnp.zeros((LANES - SSM_HEADS,), F32)
    dtb = jnp.concatenate([dt_bias[0, 0], zpad, dt_bias[0, 1], zpad]).reshape(1, 2 * LANES)
    a_log_p = jnp.pad(a_log[0], ((0, 0), (0, LANES - SSM_HEADS))).reshape(2, 1, LANES)
    dskip_e = jnp.repeat(d_skip[0], SSM_HEAD_DIM, axis=-1).reshape(2, 1, SSM_D_INNER)
    head_of_lane = jnp.arange(SSM_D_INNER, dtype=I32) // SSM_HEAD_DIM
    expand = (jnp.arange(LANES, dtype=I32)[:, None] == head_of_lane[None, :]).astype(BF16)
    lb = jax.nn.softmax(hg_lb.astype(F32), axis=0)[0].reshape(2, 1, HG_KEY)
    w_r = jnp.concatenate([w_group_router[0],
                           jnp.transpose(w_expert_router[0], (1, 0, 2)).reshape(d, MOE_EXPERTS),
                           jnp.zeros((d, LANES - MOE_GROUPS - MOE_EXPERTS), F32)], axis=1)
    b_r = jnp.concatenate([b_group_router[0], b_expert_router[0].reshape(-1),
                           jnp.zeros((LANES - MOE_GROUPS - MOE_EXPERTS,), F32)]).reshape(1, LANES)
    g_hg = g_hg_norm[0].reshape(1, HG_DIM)

    cc = jnp.concatenate([c, c_ctx[None], jnp.zeros((SUBLANES - nb - 1, d), F32)], axis=0)
    mod = _modulation(cc, w_ada[0], b_ada[0])
    h_all, h_lat = _pre_norm(x, ctx, mod, g_pre_mix)

    xbc, dt = _ssd_inputs(h_all, w_xbc, w_dt, conv_w[0], conv_b, dtb)
    y = _ssd_scan(xbc, dt, a_log_p, dskip_e, expand)

    s0 = jnp.zeros((nb, 2, HG_HEADS, HG_DIM, HG_DIM), F32)
    _, s_ctx = _hg_scan(h_all[:, :n_ctx].astype(F32), 1, w_q, w_f, w_i, lb, s0)
    o_grid, _ = _hg_scan(h_lat.reshape(nb, rows, GRID_W, d), HG_COLS_PER_STEP, w_q, w_f, w_i, lb, s_ctx)
    o = o_grid.reshape(2, nb, n_lat, d)

    xl, h2, eid, ew = _merge(x, y, o, h_all, w_zg, w_branch_ssm[0].astype(BF16), w_branch_hg[0].astype(BF16),
                             w_out[0].astype(BF16), g_ssm_norm, g_hg, g_post_mix, g_pre_ffn, mod,
                             w_r, b_r)

    n = nb * n_lat
    cap = -(-(2 * n + MOE_EXPERTS * MOE_ROWS) // MOE_ROWS) * MOE_ROWS
    n_blocks = cap // MOE_ROWS
    dest, blk = _rank(eid.reshape(n, LANES), n_blocks)
    dest_flat = dest[:, 0:2].reshape(-1)
    xs = _dispatch(dest_flat, h2.reshape(n, d), cap)
    ys = _experts(blk[:n_blocks, 0], blk[0:1, 1], xs, w_gate[0], w_up[0], w_down[0])
    out = _combine(dest_flat, ys, xl.reshape(n, d), ew.reshape(n, LANES), mod, g_post_ffn,
                   n_lat // TOK_TILE)
    return out.reshape(nb, n_lat, d)
```

```python
import functools

import jax
import jax.numpy as jnp
from jax import lax
from jax.experimental import pallas as pl
from jax.experimental.pallas import tpu as pltpu

F32 = jnp.float32
BF16 = jnp.bfloat16
I32 = jnp.int32
HIGHEST = lax.Precision.HIGHEST

D_MODEL = 1024
GRID_W = 64
RMS_EPS = 1e-6
SSM_D_INNER = 2048
SSM_HEAD_DIM = 64
SSM_HEADS = 32
SSM_GROUPS = 4
SSM_HEADS_PER_GROUP = 8
SSM_STATE = 128
SSM_GROUP_WIDTH = SSM_HEADS_PER_GROUP * SSM_HEAD_DIM
SSM_CONV_DIM = SSM_D_INNER + 2 * SSM_GROUPS * SSM_STATE
HG_HEADS = 8
HG_DIM = 128
HG_KEY = HG_HEADS * HG_DIM
MOE_GROUPS = 4
MOE_EXPERTS_PER_GROUP = 8
MOE_EXPERTS = 32
MOE_D_FF = 512

LANES = 128
SUBLANES = 8
VMEM_LIMIT_BYTES = 56 * 1024 * 1024

TOK_TILE = 256
SSD_CHUNK = 128
HG_CHUNK = 128
HG_SUB = 16
HG_FACTOR_LOG2_RANGE = 110.0
HG_COLS_PER_STEP = 8
HG_COLS_PER_PASS = 4
MOE_ROWS = 256
RANK_TILE = 512
DMA_UNROLL = 8


def _cparams(n_axes):
    return pltpu.CompilerParams(dimension_semantics=("arbitrary",) * n_axes,
                                vmem_limit_bytes=VMEM_LIMIT_BYTES)


def _sigmoid(x):
    return 0.5 * jnp.tanh(0.5 * x) + 0.5


def _silu(x):
    return x * _sigmoid(x)


def _rms(v, g):
    ms = jnp.mean(v * v, axis=-1, keepdims=True)
    return v * lax.rsqrt(ms + RMS_EPS) * g


def _dot(a, b):
    return jnp.dot(a, b, preferred_element_type=F32)


def _dot_nt(a, b):
    return lax.dot_general(a, b, (((1,), (1,)), ((), ())), preferred_element_type=F32)


def _dot_tn(a, b):
    return lax.dot_general(a, b, (((0,), (0,)), ((), ())), preferred_element_type=F32)


def _mod_kernel(c_ref, w_ref, b_ref, o_ref):
    s = _silu(c_ref[...])
    o_ref[...] = jnp.dot(s, w_ref[...], precision=HIGHEST, preferred_element_type=F32) + b_ref[...]


def _modulation(cc, w_ada, b_ada):
    n = w_ada.shape[1]
    tn = 1024
    return pl.pallas_call(
        _mod_kernel,
        grid=(n // tn,),
        in_specs=[pl.BlockSpec((SUBLANES, D_MODEL), lambda j: (0, 0)),
                  pl.BlockSpec((D_MODEL, tn), lambda j: (0, j)),
                  pl.BlockSpec((1, tn), lambda j: (0, j))],
        out_specs=pl.BlockSpec((SUBLANES, tn), lambda j: (0, j)),
        out_shape=jax.ShapeDtypeStruct((SUBLANES, n), F32),
        compiler_params=_cparams(1),
        name="adaln_mod",
    )(cc, w_ada, b_ada.reshape(1, n))


def _norm_kernel(x_ref, ctx_ref, mod_ref, g_ref, all_ref, lat_ref, *, ctx_row):
    b = pl.program_id(0)
    i = pl.program_id(1)
    g = g_ref[...]

    def nm(v, row):
        sh = mod_ref[pl.ds(row, 1), pl.ds(0, D_MODEL)]
        sc = mod_ref[pl.ds(row, 1), pl.ds(D_MODEL, D_MODEL)]
        return _rms(v, g) * (1.0 + sc) + sh

    @pl.when(i == 0)
    def _():
        all_ref[...] = nm(ctx_ref[...], ctx_row).astype(BF16)

    @pl.when(i > 0)
    def _():
        h = nm(x_ref[...], b)
        all_ref[...] = h.astype(BF16)
        lat_ref[...] = h


def _pre_norm(x, ctx, mod, g):
    nb, n_lat, d = x.shape
    n_ctx = ctx.shape[1]
    assert n_ctx == TOK_TILE and n_lat % TOK_TILE == 0
    nt = n_lat // TOK_TILE
    return pl.pallas_call(
        functools.partial(_norm_kernel, ctx_row=nb),
        grid=(nb, nt + 1),
        in_specs=[pl.BlockSpec((None, TOK_TILE, d), lambda b, i: (b, jnp.maximum(i - 1, 0), 0)),
                  pl.BlockSpec((None, TOK_TILE, d), lambda b, i: (b, 0, 0)),
                  pl.BlockSpec(mod.shape, lambda b, i: (0, 0)),
                  pl.BlockSpec((1, d), lambda b, i: (0, 0))],
        out_specs=[pl.BlockSpec((None, TOK_TILE, d), lambda b, i: (b, i, 0)),
                   pl.BlockSpec((None, TOK_TILE, d), lambda b, i: (b, jnp.maximum(i - 1, 0), 0))],
        out_shape=[jax.ShapeDtypeStruct((nb, n_ctx + n_lat, d), BF16),
                   jax.ShapeDtypeStruct((nb, n_lat, d), F32)],
        compiler_params=_cparams(2),
        name="pre_norm",
    )(x, ctx, mod, g)


XBC_COL_TILE = 512


def _xbc_kernel(h_ref, hp_ref, hn_ref, w_ref, wdt_ref, cw_ref, cb_ref, dtb_ref, xbc_ref, dt_ref,
                raw_s, *, n_tiles):
    i = pl.program_id(1)
    h = h_ref[...]
    tm = h.shape[0]
    halo = hp_ref.shape[0]
    prev_ok = (i >= 2).astype(BF16)
    next_ok = jnp.logical_and(i >= 1, i < n_tiles - 1).astype(BF16)
    rows = jnp.concatenate([hp_ref[...] * prev_ok, h, hn_ref[...] * next_ok], axis=0)
    for c in range(SSM_CONV_DIM // XBC_COL_TILE):
        sl = slice(c * XBC_COL_TILE, (c + 1) * XBC_COL_TILE)
        raw_s[...] = _dot(rows, w_ref[:, sl])
        up = raw_s[halo - 1:halo - 1 + tm, :]
        mid = raw_s[halo:halo + tm, :]
        dn = raw_s[halo + 1:halo + 1 + tm, :]
        conv = cw_ref[0:1, sl] * up + cw_ref[1:2, sl] * mid + cw_ref[2:3, sl] * dn + cb_ref[:, sl]
        xbc_ref[:, sl] = _silu(conv).astype(BF16)
    u = _dot(h, wdt_ref[...]) + dtb_ref[...]
    dt_ref[...] = jnp.maximum(u, 0.0) + jnp.log(1.0 + jnp.exp(-jnp.abs(u)))


def _ssd_inputs(h_all, w_xbc, w_dt, conv_w, conv_b, dt_bias):
    nb, ltot, d = h_all.shape
    nt = ltot // TOK_TILE
    halo = 2 * SUBLANES
    n_halo = ltot // halo
    per = TOK_TILE // halo
    return pl.pallas_call(
        functools.partial(_xbc_kernel, n_tiles=nt),
        grid=(nb, nt),
        in_specs=[pl.BlockSpec((None, TOK_TILE, d), lambda b, i: (b, i, 0)),
                  pl.BlockSpec((None, halo, d), lambda b, i: (b, jnp.maximum(i * per - 1, 0), 0)),
                  pl.BlockSpec((None, halo, d), lambda b, i: (b, jnp.minimum((i + 1) * per, n_halo - 1), 0)),
                  pl.BlockSpec(w_xbc.shape, lambda b, i: (0, 0)),
                  pl.BlockSpec(w_dt.shape, lambda b, i: (0, 0)),
                  pl.BlockSpec(conv_w.shape, lambda b, i: (0, 0)),
                  pl.BlockSpec(conv_b.shape, lambda b, i: (0, 0)),
                  pl.BlockSpec(dt_bias.shape, lambda b, i: (0, 0))],
        out_specs=[pl.BlockSpec((None, TOK_TILE, SSM_CONV_DIM), lambda b, i: (b, i, 0)),
                   pl.BlockSpec((None, TOK_TILE, 2 * LANES), lambda b, i: (b, i, 0))],
        out_shape=[jax.ShapeDtypeStruct((nb, ltot, SSM_CONV_DIM), BF16),
                   jax.ShapeDtypeStruct((nb, ltot, 2 * LANES), F32)],
        scratch_shapes=[pltpu.VMEM((TOK_TILE + 2 * halo, XBC_COL_TILE), F32)],
        compiler_params=_cparams(2),
        name="ssd_inputs",
    )(h_all, h_all, h_all, w_xbc, w_dt, conv_w, conv_b, dt_bias)


def _ssd_kernel(xbc_ref, dt_ref, alog_ref, dskip_ref, e_ref, y_ref, state):
    d = pl.program_id(1)
    step = pl.program_id(2)
    q = SSD_CHUNK

    @pl.when(step == 0)
    def _():
        state[...] = jnp.zeros(state.shape, F32)

    lane = lax.broadcasted_iota(I32, (1, LANES), 1)
    a = jnp.where(lane < SSM_HEADS, -jnp.exp(alog_ref[...]), 0.0)
    dt = dt_ref[...]
    adt = dt * a
    r = lax.broadcasted_iota(I32, (q, q), 0)
    c = lax.broadcasted_iota(I32, (q, q), 1)
    mask = jnp.where(d == 0, r - c, c - r) >= 0
    cs = jnp.dot(mask.astype(F32), adt, precision=HIGHEST, preferred_element_type=F32)
    tot = jnp.sum(adt, axis=0, keepdims=True)
    cs_t = cs.T
    lhs = jnp.concatenate([dt, jnp.exp(cs), jnp.exp(tot - cs),
                           jnp.broadcast_to(jnp.exp(tot), (SUBLANES, LANES))], axis=0).astype(BF16)
    lane_h = lax.broadcasted_iota(I32, (q, LANES), 1)
    for g in range(SSM_GROUPS):
        gs = slice(g * SSM_GROUP_WIDTH, (g + 1) * SSM_GROUP_WIDTH)
        ex = _dot(lhs, e_ref[:, gs])
        dt_e, e1_e, e2_e, etot_e = ex[0:q], ex[q:2 * q], ex[2 * q:3 * q], ex[3 * q:3 * q + 1]
        x = xbc_ref[:, gs].astype(F32)
        bg = xbc_ref[:, SSM_D_INNER + g * SSM_STATE:SSM_D_INNER + (g + 1) * SSM_STATE]
        cg = xbc_ref[:, SSM_D_INNER + (SSM_GROUPS + g) * SSM_STATE:
                     SSM_D_INNER + (SSM_GROUPS + g + 1) * SSM_STATE]
        xdt = x * dt_e
        cb = _dot_nt(cg, bg)
        s_in = state[g]
        y_off = _dot(cg, s_in.astype(BF16)) * e1_e
        state[g] = s_in * etot_e + _dot_tn(bg, (xdt * e2_e).astype(BF16))
        parts = []
        for m in range(SSM_HEADS_PER_GROUP // 2):
            xp = xdt[:, m * LANES:(m + 1) * LANES]
            acc = None
            for half in range(2):
                hd = g * SSM_HEADS_PER_GROUP + 2 * m + half
                col = cs[:, hd:hd + 1]
                rowv = cs_t[hd:hd + 1, :]
                decay = jnp.exp(jnp.minimum(col - rowv, 0.0))
                w = jnp.where(mask, cb * decay, 0.0).astype(BF16)
                sel = (lane_h < SSM_HEAD_DIM) if half == 0 else (lane_h >= SSM_HEAD_DIM)
                t = _dot(w, jnp.where(sel, xp, 0.0).astype(BF16))
                acc = t if acc is None else acc + t
            parts.append(acc)
        y = jnp.concatenate(parts, axis=1) + y_off + x * dskip_ref[:, gs]
        y_ref[:, gs] = y.astype(BF16)


def _ssd_scan(xbc, dt, a_log_p, dskip_e, expand):
    nb, ltot, _ = xbc.shape
    q = SSD_CHUNK
    nc = ltot // q
    n_ctx_chunks = TOK_TILE // q

    def chunk(d, s):
        rev = jnp.where(s < n_ctx_chunks, n_ctx_chunks - 1 - s, nc + n_ctx_chunks - 1 - s)
        return jnp.where(d == 0, s, rev)

    return pl.pallas_call(
        _ssd_kernel,
        grid=(nb, 2, nc),
        in_specs=[pl.BlockSpec((None, q, SSM_CONV_DIM), lambda b, d, s: (b, chunk(d, s), 0)),
                  pl.BlockSpec((None, q, LANES), lambda b, d, s: (b, chunk(d, s), d)),
                  pl.BlockSpec((None, 1, LANES), lambda b, d, s: (d, 0, 0)),
                  pl.BlockSpec((None, 1, SSM_D_INNER), lambda b, d, s: (d, 0, 0)),
                  pl.BlockSpec(expand.shape, lambda b, d, s: (0, 0))],
        out_specs=pl.BlockSpec((None, None, q, SSM_D_INNER), lambda b, d, s: (d, b, chunk(d, s), 0)),
        out_shape=jax.ShapeDtypeStruct((2, nb, ltot, SSM_D_INNER), BF16),
        scratch_shapes=[pltpu.VMEM((SSM_GROUPS, SSM_STATE, SSM_GROUP_WIDTH), F32)],
        compiler_params=_cparams(3),
        name="ssd_scan",
    )(xbc, dt, a_log_p, dskip_e, expand)


def _hg_intra(q_s, k_s, b_s, r0, off):
    c = HG_CHUNK
    cols = pl.ds(off, HG_DIM)
    lane8 = lax.broadcasted_iota(I32, (SUBLANES, c), 1)
    lane_sub = lax.broadcasted_iota(I32, (HG_SUB, c), 1)
    rows = []
    for i in range(c // HG_SUB):
        lo = r0 + i * HG_SUB
        bi = b_s[lo:lo + HG_SUB, cols]
        qi = q_s[lo:lo + HG_SUB, cols]
        top = jnp.zeros((SUBLANES, c), F32)
        bot = jnp.zeros((SUBLANES, c), F32)
        for j in range(HG_SUB):
            s = lo + j
            at = i * HG_SUB + j
            brow = b_s[s:s + 1, cols]
            krow = k_s[s:s + 1, cols]
            if j < SUBLANES:
                p = qi * krow * jnp.exp2(bi - brow)
                top = jnp.where(lane8 == at, jnp.sum(p[0:SUBLANES], axis=1, keepdims=True), top)
                bot = jnp.where(lane8 == at, jnp.sum(p[SUBLANES:], axis=1, keepdims=True), bot)
            else:
                p = qi[SUBLANES:] * krow * jnp.exp2(bi[SUBLANES:] - brow)
                bot = jnp.where(lane8 == at, jnp.sum(p, axis=1, keepdims=True), bot)
        a_row = jnp.concatenate([top, bot], axis=0)
        if i > 0:
            bref = b_s[lo - 1:lo, cols]
            qt = qi * jnp.exp2(bi - bref)
            kt = k_s[r0:lo, cols] * jnp.exp2(bref - b_s[r0:lo, cols])
            ktp = jnp.concatenate([kt, jnp.zeros((c - i * HG_SUB, HG_DIM), F32)], axis=0)
            a_off = _dot_nt(qt.astype(BF16), ktp.astype(BF16))
            a_row = jnp.where(lane_sub < i * HG_SUB, a_off, a_row)
        rows.append(a_row)
    a = jnp.concatenate(rows, axis=0)
    r64 = lax.broadcasted_iota(I32, (c, c), 0)
    c64 = lax.broadcasted_iota(I32, (c, c), 1)
    return jnp.where(r64 >= c64, a, 0.0)


def _hg_kernel(h_ref, wq_ref, wf_ref, wi_ref, lb_ref, sin_ref, o_ref, sout_ref,
               pin_s, pout_s, q_s, k_s, b_s, v_s, qe_s, kd_s, ki_s, dec_s, o_s, state, *, log2_rows,
               log2_ncol, pass_rows):
    d = pl.program_id(1)
    step = pl.program_id(2)
    nsteps = pl.num_programs(2)
    t_all = 1 << (log2_rows + log2_ncol)
    c = HG_CHUNK
    nch = pass_rows // c

    def source_row(p):
        src = ((p & ((1 << log2_rows) - 1)) << log2_ncol) + (p >> log2_rows)
        return jnp.where(d == 0, src, t_all - 1 - src)

    @pl.when(step == 0)
    def _():
        state[...] = sin_ref[...]
        shape = (t_all, t_all)
        pin_s[...] = (lax.broadcasted_iota(I32, shape, 1)
                      == source_row(lax.broadcasted_iota(I32, shape, 0))).astype(BF16)
        pout_s[...] = (lax.broadcasted_iota(I32, shape, 0)
                       == source_row(lax.broadcasted_iota(I32, shape, 1))).astype(BF16)

    r64 = lax.broadcasted_iota(I32, (c, c), 0)
    c64 = lax.broadcasted_iota(I32, (c, c), 1)
    tril = (r64 >= c64).astype(BF16)
    h_rows =h_ref[...].reshape(t_all, HG_KEY).astype(BF16)

    for ps in range(t_all // pass_rows):
        prs = slice(ps * pass_rows, (ps + 1) * pass_rows)
        hb = _dot(pin_s[prs, :], h_rows).astype(BF16)
        q_s[...] = _silu(_dot(hb, wq_ref[...]))
        fz = _dot(hb, wf_ref[...])
        t = jnp.exp(-jnp.abs(fz))
        r = 1.0 / (1.0 + t)
        big, small = r, t * r
        pos = fz >= 0.0
        lb = lb_ref[...]
        lf2 = jnp.log2(lb + (1.0 - lb) * jnp.where(pos, big, small))
        k_s[...] = (1.0 - lb) * jnp.where(pos, small, big)
        v_s[...] = _dot(hb, wi_ref[...]).astype(BF16)
        for ch in range(nch):
            rs = slice(ch * c, (ch + 1) * c)
            x = lf2[rs]
            hi = x.astype(BF16)
            rem = x - hi.astype(F32)
            mid = rem.astype(BF16)
            low = (rem - mid.astype(F32)).astype(BF16)
            b = _dot(tril, hi) + _dot(tril, mid) + _dot(tril, low)
            b_s[rs, :] = b
            btot = b[c - 1:c]
            qe_s[rs, :] = (q_s[rs, :] * jnp.exp2(b)).astype(BF16)
            kd_s[rs, :] = (k_s[rs, :] * jnp.exp2(btot - b)).astype(BF16)
            ki_s[rs, :] = (k_s[rs, :] * jnp.exp2(-b)).astype(BF16)
            dec_s[ch:ch + 1, :] = jnp.exp2(btot)
            steepest = -btot if ch == 0 else jnp.maximum(steepest, -btot)
        bounded = jnp.max(steepest) <= HG_FACTOR_LOG2_RANGE

        def head(hh, carry, factored):
            off = pl.multiple_of(hh * HG_DIM, HG_DIM)
            cols = pl.ds(off, HG_DIM)
            s_t = state[hh]
            for ch in range(nch):
                rs = slice(ch * c, (ch + 1) * c)
                if factored:
                    a = jnp.where(r64 >= c64, _dot_nt(qe_s[rs, cols], ki_s[rs, cols]), 0.0)
                else:
                    a = _hg_intra(q_s, k_s, b_s, ch * c, off)
                v = v_s[rs, cols]
                o_s[rs, cols] = _dot_nt(qe_s[rs, cols], s_t.astype(BF16)) + _dot(a.astype(BF16), v)
                s_t = s_t * dec_s[ch:ch + 1, cols] + _dot_tn(v, kd_s[rs, cols])
            state[hh] = s_t
            return carry

        @pl.when(bounded)
        def _():
            lax.fori_loop(0, HG_HEADS, functools.partial(head, factored=True), 0)

        @pl.when(jnp.logical_not(bounded))
        def _():
            lax.fori_loop(0, HG_HEADS, functools.partial(head, factored=False), 0)
        back = _dot(pout_s[:, prs], o_s[...].astype(BF16)).reshape(o_ref.shape)
        if ps == 0:
            o_ref[...] = back
        else:
            o_ref[...] += back

    @pl.when(step == nsteps - 1)
    def _():
        sout_ref[...] = state[...]


def _hg_scan(h_grid, ncol, wq, wf, wi, lb, s_in):
    if h_grid.ndim == 3:
        nb, rows, d_model = h_grid.shape
        ncol_total = 1
    else:
        nb, rows, ncol_total, d_model = h_grid.shape
    nsteps = ncol_total // ncol
    t_all = rows * ncol
    pass_rows = min(t_all, HG_COLS_PER_PASS * LANES)
    log2_rows, log2_ncol = rows.bit_length() - 1, ncol.bit_length() - 1
    assert rows == 1 << log2_rows and ncol == 1 << log2_ncol and t_all % pass_rows == 0
    assert pass_rows // HG_CHUNK <= SUBLANES

    def pos(d, s):
        return jnp.where(d == 0, s, nsteps - 1 - s)

    once = pl.Buffered(1)
    if h_grid.ndim == 3:
        h_spec = pl.BlockSpec((None, rows, d_model), lambda b, d, s: (b, 0, 0))
        o_spec = pl.BlockSpec((None, None, rows, d_model), lambda b, d, s: (d, b, 0, 0))
    else:
        h_spec = pl.BlockSpec((None, rows, ncol, d_model), lambda b, d, s: (b, 0, pos(d, s), 0))
        o_spec = pl.BlockSpec((None, None, rows, ncol, d_model), lambda b, d, s: (d, b, 0, pos(d, s), 0))
    st_spec = pl.BlockSpec((None, None, HG_HEADS, HG_DIM, HG_DIM), lambda b, d, s: (b, d, 0, 0, 0))
    wide_f32 = pltpu.VMEM((pass_rows, HG_KEY), F32)
    wide_bf16 = pltpu.VMEM((pass_rows, HG_KEY), BF16)
    perm = pltpu.VMEM((t_all, t_all), BF16)
    return pl.pallas_call(
        functools.partial(_hg_kernel, log2_rows=log2_rows, log2_ncol=log2_ncol, pass_rows=pass_rows),
        grid=(nb, 2, nsteps),
        in_specs=[h_spec,
                  pl.BlockSpec(wq.shape, lambda b, d, s: (0, 0), pipeline_mode=once),
                  pl.BlockSpec((None, d_model, HG_KEY), lambda b, d, s: (d, 0, 0), pipeline_mode=once),
                  pl.BlockSpec(wi.shape, lambda b, d, s: (0, 0), pipeline_mode=once),
                  pl.BlockSpec((None, 1, HG_KEY), lambda b, d, s: (d, 0, 0)),
                  st_spec],
        out_specs=[o_spec, st_spec],
        out_shape=[jax.ShapeDtypeStruct((2,) + h_grid.shape, F32),
                   jax.ShapeDtypeStruct(s_in.shape, F32)],
        scratch_shapes=[perm, perm, wide_f32, wide_f32, wide_f32, wide_bf16, wide_bf16, wide_bf16, wide_bf16,
                        pltpu.VMEM((SUBLANES, HG_KEY), F32), wide_f32,
                        pltpu.VMEM((HG_HEADS, HG_DIM, HG_DIM), F32)],
        compiler_params=_cparams(3),
        name="hgrn2_scan",
    )(h_grid, wq, wf, wi, lb, s_in)


def _merge_kernel(x_ref, yf_ref, yb_ref, of_ref, ob_ref, h_ref, wzg_ref, wbs_ref, wbh_ref, wo_ref,
                  gs_ref, gh_ref, gpm_ref, gpf_ref, mod_ref, wr_ref, br_ref,
                  xl_ref, h2_ref, eid_ref, ew_ref):
    b = pl.program_id(0)
    tm = x_ref.shape[0]
    d = D_MODEL
    h = h_ref[...]
    z = _dot(h, wzg_ref[:, 0:SSM_D_INNER])
    yz = (yf_ref[...].astype(F32) + yb_ref[...].astype(F32)) * _silu(z)
    parts = []
    for g in range(SSM_GROUPS):
        gsl = slice(g * SSM_GROUP_WIDTH, (g + 1) * SSM_GROUP_WIDTH)
        parts.append(_rms(yz[:, gsl], gs_ref[:, gsl]).astype(BF16))
    br_ssm = _dot(jnp.concatenate(parts, axis=1), wbs_ref[...])

    o = of_ref[...].astype(F32) + ob_ref[...].astype(F32)
    og = _dot(h, wzg_ref[:, SSM_D_INNER:SSM_D_INNER + d])
    parts = []
    for hh in range(HG_HEADS):
        hsl = slice(hh * HG_DIM, (hh + 1) * HG_DIM)
        parts.append((_rms(o[:, hsl], gh_ref[...]) * _silu(og[:, hsl])).astype(BF16))
    br_hg = _dot(jnp.concatenate(parts, axis=1), wbh_ref[...])

    gm = _dot(h, wzg_ref[:, SSM_D_INNER + d:SSM_D_INNER + 2 * d])
    gh = _dot(h, wzg_ref[:, SSM_D_INNER + 2 * d:SSM_D_INNER + 3 * d])
    mixed = _sigmoid(gm) * br_ssm + _sigmoid(gh) * br_hg
    mix = _dot(mixed.astype(BF16), wo_ref[...])

    gt1 = mod_ref[pl.ds(b, 1), pl.ds(2 * d, d)]
    sh2 = mod_ref[pl.ds(b, 1), pl.ds(3 * d, d)]
    sc2 = mod_ref[pl.ds(b, 1), pl.ds(4 * d, d)]
    xl = x_ref[...] + gt1 * _rms(mix, gpm_ref[...])
    xl_ref[...] = xl
    h2 = _rms(xl, gpf_ref[...]) * (1.0 + sc2) + sh2
    h2_ref[...] = h2

    logits = jnp.dot(h2, wr_ref[...], precision=HIGHEST, preferred_element_type=F32) + br_ref[...]
    lane = lax.broadcasted_iota(I32, (tm, LANES), 1)
    lane_f = lane.astype(F32)
    neg = jnp.float32(-jnp.inf)
    far = jnp.float32(2 * LANES)
    gl = jnp.where(lane < MOE_GROUPS, logits, neg)
    gmax = jnp.max(gl, axis=1, keepdims=True)
    p_grp = 1.0 / jnp.sum(jnp.exp(gl - gmax), axis=1, keepdims=True)
    grp = jnp.min(jnp.where(gl == gmax, lane_f, far), axis=1, keepdims=True)
    lo = MOE_GROUPS + MOE_EXPERTS_PER_GROUP * grp
    el = jnp.where(jnp.logical_and(lane_f >= lo, lane_f < lo + MOE_EXPERTS_PER_GROUP), logits, neg)
    m1 = jnp.max(el, axis=1, keepdims=True)
    i1 = jnp.min(jnp.where(el == m1, lane_f, far), axis=1, keepdims=True)
    el2 = jnp.where(lane_f == i1, neg, el)
    m2 = jnp.max(el2, axis=1, keepdims=True)
    i2 = jnp.min(jnp.where(el2 == m2, lane_f, far), axis=1, keepdims=True)
    e2 = jnp.exp(m2 - m1)
    w1 = p_grp / (1.0 + e2)
    w2 = p_grp * e2 / (1.0 + e2)
    eid = jnp.where(lane == 0, i1, i2) - float(MOE_GROUPS)
    eid_ref[...] = jnp.where(lane < 2, eid, 0.0).astype(I32)
    ew_ref[...] = jnp.where(lane == 0, w1, jnp.where(lane == 1, w2, 0.0))


def _merge(x, y, o, h_all, w_zg, w_bs, w_bh, w_o, g_ssm, g_hg, g_pm, g_pf, mod, w_r, b_r):
    nb, n_lat, d = x.shape
    tm = TOK_TILE
    ctx_tiles = (y.shape[2] - n_lat) // tm

    def full(a):
        return pl.BlockSpec(a.shape, lambda b, i: (0,) * a.ndim, pipeline_mode=pl.Buffered(1))

    tok = lambda w: pl.BlockSpec((None, tm, w), lambda b, i: (b, i, 0))
    return pl.pallas_call(
        _merge_kernel,
        grid=(nb, n_lat // tm),
        in_specs=[tok(d),
                  pl.BlockSpec((None, None, tm, SSM_D_INNER), lambda b, i: (0, b, i + ctx_tiles, 0)),
                  pl.BlockSpec((None, None, tm, SSM_D_INNER), lambda b, i: (1, b, i + ctx_tiles, 0)),
                  pl.BlockSpec((None, None, tm, d), lambda b, i: (0, b, i, 0)),
                  pl.BlockSpec((None, None, tm, d), lambda b, i: (1, b, i, 0)),
                  pl.BlockSpec((None, tm, d), lambda b, i: (b, i + ctx_tiles, 0)),
                  full(w_zg), full(w_bs), full(w_bh), full(w_o), full(g_ssm), full(g_hg), full(g_pm),
                  full(g_pf), full(mod), full(w_r), full(b_r)],
        out_specs=[tok(d), tok(d), tok(LANES), tok(LANES)],
        out_shape=[jax.ShapeDtypeStruct((nb, n_lat, d), F32),
                   jax.ShapeDtypeStruct((nb, n_lat, d), F32),
                   jax.ShapeDtypeStruct((nb, n_lat, LANES), I32),
                   jax.ShapeDtypeStruct((nb, n_lat, LANES), F32)],
        compiler_params=_cparams(2),
        name="merge_route",
    )(x, y, y, o, o, h_all, w_zg, w_bs, w_bh, w_o, g_ssm, g_hg, g_pm, g_pf, mod, w_r, b_r)


def _rank_kernel(eid_ref, dest_ref, blk_ref, fill_ref, cnt, base):
    p = pl.program_id(0)
    i = pl.program_id(1)
    t = eid_ref.shape[0]
    lane = lax.broadcasted_iota(I32, (t, LANES), 1)
    e = eid_ref[...]
    oh0 = (lane == e[:, 0:1]).astype(F32)
    oh1 = (lane == e[:, 1:2]).astype(F32)
    c0 = jnp.sum(oh0, axis=0, keepdims=True)
    c1 = jnp.sum(oh1, axis=0, keepdims=True)

    @pl.when(jnp.logical_and(p == 0, i == 0))
    def _():
        cnt[...] = jnp.zeros(cnt.shape, F32)

    @pl.when(p == 0)
    def _():
        cnt[...] = cnt[...] + c0 + c1

    @pl.when(jnp.logical_and(p == 1, i == 0))
    def _():
        lane1 = lax.broadcasted_iota(I32, (SUBLANES, LANES), 1)
        padded = jnp.floor((cnt[...] + (MOE_ROWS - 1)) / MOE_ROWS) * MOE_ROWS
        padded = jnp.broadcast_to(jnp.where(lane1[0:1] < MOE_EXPERTS, padded, 0.0), (SUBLANES, LANES))
        rr = lax.broadcasted_iota(I32, (LANES, LANES), 0)
        cc = lax.broadcasted_iota(I32, (LANES, LANES), 1)
        start = jnp.dot(padded, (rr < cc).astype(F32), precision=HIGHEST, preferred_element_type=F32)
        base[...] = start[0:1]
        pad_end = start[0:1] + padded[0:1]
        pad_end = jnp.where(lane1[0:1] < MOE_EXPERTS, pad_end, jnp.float32(2.0 ** 30))
        nblk = blk_ref.shape[0]
        bstart = lax.broadcasted_iota(I32, (nblk, LANES), 0).astype(F32) * MOE_ROWS
        owner = jnp.sum((pad_end <= bstart).astype(F32), axis=1, keepdims=True)
        owner = jnp.minimum(owner, MOE_EXPERTS - 1)
        total = jnp.max(jnp.where(lane1[0:1] == MOE_EXPERTS - 1, pad_end, 0.0), axis=1, keepdims=True)
        lane_b = lax.broadcasted_iota(I32, (nblk, LANES), 1)
        blk_ref[...] = jnp.where(lane_b == 0, owner, total / MOE_ROWS).astype(I32)
        sub = lax.broadcasted_iota(I32, (SUBLANES, LANES), 0)
        fill_ref[...] = jnp.where(sub == 0, start + cnt[...], start + padded).astype(I32)

    @pl.when(p == 1)
    def _():
        rr = lax.broadcasted_iota(I32, (t, t), 0)
        cc = lax.broadcasted_iota(I32, (t, t), 1)
        before = (rr > cc).astype(BF16)
        b0 = base[...]
        r0 = _dot(before, oh0.astype(BF16)) + b0
        r1 = _dot(before, oh1.astype(BF16)) + b0 + c0
        d0 = jnp.sum(oh0 * r0, axis=1, keepdims=True)
        d1 = jnp.sum(oh1 * r1, axis=1, keepdims=True)
        dest_ref[...] = jnp.where(lane == 0, d0, jnp.where(lane == 1, d1, 0.0)).astype(I32)
        base[...] = b0 + c0 + c1


def _rank(eid, n_blocks):
    n = eid.shape[0]
    t = RANK_TILE
    nblk_pad = -(-n_blocks // SUBLANES) * SUBLANES
    return pl.pallas_call(
        _rank_kernel,
        grid=(2, n // t),
        in_specs=[pl.BlockSpec((t, LANES), lambda p, i: (i, 0))],
        out_specs=[pl.BlockSpec((t, LANES), lambda p, i: (p * i, 0)),
                   pl.BlockSpec((nblk_pad, LANES), lambda p, i: (0, 0)),
                   pl.BlockSpec((SUBLANES, LANES), lambda p, i: (0, 0))],
        out_shape=[jax.ShapeDtypeStruct((n, LANES), I32),
                   jax.ShapeDtypeStruct((nblk_pad, LANES), I32),
                   jax.ShapeDtypeStruct((SUBLANES, LANES), I32)],
        scratch_shapes=[pltpu.VMEM((1, LANES), F32), pltpu.VMEM((1, LANES), F32)],
        compiler_params=_cparams(2),
        name="moe_rank",
    )(eid)


def _row_copy(src, s, dst, t, sem):
    return pltpu.make_async_copy(src.at[pl.ds(s, 1)], dst.at[pl.ds(t, 1)], sem)


def _dispatch_kernel(dest_ref, fill_ref, h_ref, out_ref, zero_row, sem):
    i = pl.program_id(0)
    t = h_ref.shape[0]
    blk_rows = zero_row.shape[0]

    @pl.when(i == 0)
    def _():
        zero_row[...] = jnp.zeros(zero_row.shape, F32)
        first_free = fill_ref[2 * MOE_EXPERTS - 1] // blk_rows
        n_blocks = out_ref.shape[0] // blk_rows

        def blk_copy(b):
            return pltpu.make_async_copy(zero_row, out_ref.at[pl.ds(b * blk_rows, blk_rows)], sem)

        def fill_blk(b, c):
            blk_copy(b).start()
            return c

        def done_blk(b, c):
            blk_copy(0).wait()
            return c

        lax.fori_loop(first_free, n_blocks, fill_blk, 0)
        lax.fori_loop(first_free, n_blocks, done_blk, 0)

        def per_expert(e, carry):
            lo = fill_ref[e]
            hi = fill_ref[MOE_EXPERTS + e]

            def fill(r, c):
                _row_copy(zero_row, 0, out_ref, r, sem).start()
                return c

            def done(r, c):
                _row_copy(zero_row, 0, out_ref, 0, sem).wait()
                return c

            lax.fori_loop(lo, hi, fill, 0)
            lax.fori_loop(lo, hi, done, 0)
            return carry

        lax.fori_loop(0, MOE_EXPERTS, per_expert, 0)

    def start(r, carry):
        base = (i * t + r) * 2
        _row_copy(h_ref, r, out_ref, dest_ref[base], sem).start()
        _row_copy(h_ref, r, out_ref, dest_ref[base + 1], sem).start()
        return carry

    lax.fori_loop(0, t, start, 0, unroll=DMA_UNROLL)

    def wait(r, carry):
        _row_copy(h_ref, 0, out_ref, 0, sem).wait()
        _row_copy(h_ref, 0, out_ref, 0, sem).wait()
        return carry

    lax.fori_loop(0, t, wait, 0, unroll=DMA_UNROLL)


def _dispatch(dest_flat, fill_flat, h2, cap):
    n, d = h2.shape
    t = TOK_TILE
    return pl.pallas_call(
        _dispatch_kernel,
        grid_spec=pltpu.PrefetchScalarGridSpec(
            num_scalar_prefetch=2,
            grid=(n // t,),
            in_specs=[pl.BlockSpec((t, d), lambda i, dest, fill: (i, 0))],
            out_specs=pl.BlockSpec(memory_space=pl.ANY),
            scratch_shapes=[pltpu.VMEM((MOE_ROWS, d), F32), pltpu.SemaphoreType.DMA]),
        out_shape=jax.ShapeDtypeStruct((cap, d), F32),
        compiler_params=_cparams(1),
        name="moe_dispatch",
    )(dest_flat, fill_flat, h2)


def _expert_kernel(be_ref, na_ref, x_ref, wg_ref, wu_ref, wd_ref, y_ref, wg_s, wu_s, wd_s):
    i = pl.program_id(0)
    e = be_ref[i]
    prev = be_ref[jnp.maximum(i - 1, 0)]

    @pl.when(jnp.logical_or(i == 0, e != prev))
    def _():
        wg_s[...] = wg_ref[...].astype(BF16)
        wu_s[...] = wu_ref[...].astype(BF16)
        wd_s[...] = wd_ref[...].astype(BF16)

    @pl.when(i < na_ref[0])
    def _():
        x = x_ref[...].astype(BF16)
        hid = _silu(_dot(x, wg_s[...])) * _dot(x, wu_s[...])
        y_ref[...] = _dot(hid.astype(BF16), wd_s[...])

    @pl.when(i >= na_ref[0])
    def _():
        y_ref[...] = jnp.zeros(y_ref.shape, F32)


def _experts(blk_e, n_act, xs, w_gate, w_up, w_down):
    cap, d = xs.shape
    r = MOE_ROWS
    ff = w_gate.shape[2]
    return pl.pallas_call(
        _expert_kernel,
        grid_spec=pltpu.PrefetchScalarGridSpec(
            num_scalar_prefetch=2,
            grid=(cap // r,),
            in_specs=[pl.BlockSpec((r, d), lambda i, be, na: (jnp.where(i < na[0], i, 0), 0)),
                      pl.BlockSpec((None, d, ff), lambda i, be, na: (be[i], 0, 0)),
                      pl.BlockSpec((None, d, ff), lambda i, be, na: (be[i], 0, 0)),
                      pl.BlockSpec((None, ff, d), lambda i, be, na: (be[i], 0, 0))],
            out_specs=pl.BlockSpec((r, d), lambda i, be, na: (i, 0)),
            scratch_shapes=[pltpu.VMEM((d, ff), BF16), pltpu.VMEM((d, ff), BF16),
                            pltpu.VMEM((ff, d), BF16)]),
        out_shape=jax.ShapeDtypeStruct((cap, d), F32),
        compiler_params=_cparams(1),
        name="moe_experts",
    )(blk_e, n_act, xs, w_gate, w_up, w_down)


def _combine_kernel(dest_ref, ys_ref, xl_ref, ew_ref, mod_ref, g_ref, out_ref, buf, sems, *, tiles_per_batch):
    i = pl.program_id(0)
    n_steps = pl.num_programs(0)
    t = xl_ref.shape[0]

    def gather(tile, slot):
        def start(r, carry):
            base = (tile * t + r) * 2
            _row_copy(ys_ref, dest_ref[base], buf.at[slot, 0], r, sems.at[slot]).start()
            _row_copy(ys_ref, dest_ref[base + 1], buf.at[slot, 1], r, sems.at[slot]).start()
            return carry

        lax.fori_loop(0, t, start, 0, unroll=DMA_UNROLL)

    @pl.when(i == 0)
    def _():
        gather(0, 0)

    @pl.when(i + 1 < n_steps)
    def _():
        gather(i + 1, (i + 1) % 2)

    slot = i % 2

    def wait(r, carry):
        _row_copy(ys_ref, 0, buf.at[slot, 0], 0, sems.at[slot]).wait()
        _row_copy(ys_ref, 0, buf.at[slot, 1], 0, sems.at[slot]).wait()
        return carry

    lax.fori_loop(0, t, wait, 0, unroll=DMA_UNROLL)
    b = i // tiles_per_batch
    ew = ew_ref[...]
    ffn = buf[slot, 0] * ew[:, 0:1] + buf[slot, 1] * ew[:, 1:2]
    gt2 = mod_ref[pl.ds(b, 1), pl.ds(5 * D_MODEL, D_MODEL)]
    out_ref[...] = xl_ref[...] + gt2 * _rms(ffn, g_ref[...])


def _combine(dest_flat, ys, xl, ew, mod, g, tiles_per_batch):
    n, d = xl.shape
    t = TOK_TILE
    return pl.pallas_call(
        functools.partial(_combine_kernel, tiles_per_batch=tiles_per_batch),
        grid_spec=pltpu.PrefetchScalarGridSpec(
            num_scalar_prefetch=1,
            grid=(n // t,),
            in_specs=[pl.BlockSpec(memory_space=pl.ANY),
                      pl.BlockSpec((t, d), lambda i, dest: (i, 0)),
                      pl.BlockSpec((t, LANES), lambda i, dest: (i, 0)),
                      pl.BlockSpec(mod.shape, lambda i, dest: (0, 0)),
                      pl.BlockSpec(g.shape, lambda i, dest: (0, 0))],
            out_specs=pl.BlockSpec((t, d), lambda i, dest: (i, 0)),
            scratch_shapes=[pltpu.VMEM((2, 2, t, d), F32), pltpu.SemaphoreType.DMA((2,))]),
        out_shape=jax.ShapeDtypeStruct((n, d), F32),
        compiler_params=_cparams(1),
        name="moe_combine",
    )(dest_flat, ys, xl, ew, mod, g)


def kernel(x, c, ctx, c_ctx, w_ada, b_ada, g_pre_mix, g_post_mix, g_pre_ffn, g_post_ffn, w_in,
           conv_w, conv_b, dt_bias, a_log, d_skip, g_ssm_norm, hg_lb, g_hg_norm, w_branch_ssm,
           w_branch_hg, w_out, w_group_router, b_group_router, w_expert_router, b_expert_router,
           w_gate, w_up, w_down):
    nb, n_lat, d = x.shape
    n_ctx = ctx.shape[1]
    rows = n_lat // GRID_W
    assert w_ada.shape[0] == 1 and d == D_MODEL and nb + 1 <= SUBLANES
    assert rows % HG_CHUNK == 0 and n_ctx % (2 * HG_CHUNK) == 0

    w = w_in[0]
    o0 = 0
    w_z = w[:, o0:o0 + SSM_D_INNER]; o0 += SSM_D_INNER
    w_xbc = w[:, o0:o0 + SSM_CONV_DIM].astype(BF16); o0 += SSM_CONV_DIM
    w_dtr = w[:, o0:o0 + 2 * SSM_HEADS]; o0 += 2 * SSM_HEADS
    w_q = w[:, o0:o0 + HG_KEY].astype(BF16); o0 += HG_KEY
    w_f = jnp.stack([w[:, o0:o0 + HG_KEY], w[:, o0 + HG_KEY:o0 + 2 * HG_KEY]]).astype(BF16); o0 += 2 * HG_KEY
    w_i = w[:, o0:o0 + HG_KEY].astype(BF16); o0 += HG_KEY
    w_zg = jnp.concatenate([w_z, w[:, o0:o0 + 3 * d]], axis=1).astype(BF16)
    pad = jnp.zeros((d, LANES - SSM_HEADS), F32)
    w_dt = jnp.concatenate([w_dtr[:, :SSM_HEADS], pad, w_dtr[:, SSM_HEADS:], pad], axis=1).astype(BF16)
    zpad = jnp.zeros((LANES - SSM_HEADS,), F32)
    dtb = jnp.concatenate([dt_bias[0, 0], zpad, dt_bias[0, 1], zpad]).reshape(1, 2 * LANES)
    a_log_p = jnp.pad(a_log[0], ((0, 0), (0, LANES - SSM_HEADS))).reshape(2, 1, LANES)
    dskip_e = jnp.repeat(d_skip[0], SSM_HEAD_DIM, axis=-1).reshape(2, 1, SSM_D_INNER)
    head_of_lane = jnp.arange(SSM_D_INNER, dtype=I32) // SSM_HEAD_DIM
    expand = (jnp.arange(LANES, dtype=I32)[:, None] == head_of_lane[None, :]).astype(BF16)
    lb = jax.nn.softmax(hg_lb.astype(F32), axis=0)[0].reshape(2, 1, HG_KEY)
    w_r = jnp.concatenate([w_group_router[0],
                           jnp.transpose(w_expert_router[0], (1, 0, 2)).reshape(d, MOE_EXPERTS),
                           jnp.zeros((d, LANES - MOE_GROUPS - MOE_EXPERTS), F32)], axis=1)
    b_r = jnp.concatenate([b_group_router[0], b_expert_router[0].reshape(-1),
                           jnp.zeros((LANES - MOE_GROUPS - MOE_EXPERTS,), F32)]).reshape(1, LANES)
    g_hg = g_hg_norm[0].reshape(1, HG_DIM)

    cc = jnp.concatenate([c, c_ctx[None], jnp.zeros((SUBLANES - nb - 1, d), F32)], axis=0)
    mod = _modulation(cc, w_ada[0], b_ada[0])
    h_all, h_lat = _pre_norm(x, ctx, mod, g_pre_mix)

    xbc, dt = _ssd_inputs(h_all, w_xbc, w_dt, conv_w[0], conv_b, dtb)
    y = _ssd_scan(xbc, dt, a_log_p, dskip_e, expand)

    s0 = jnp.zeros((nb, 2, HG_HEADS, HG_DIM, HG_DIM), F32)
    _, s_ctx = _hg_scan(h_all[:, :n_ctx].astype(F32), 1, w_q, w_f, w_i, lb, s0)
    o_grid, _ = _hg_scan(h_lat.reshape(nb, rows, GRID_W, d), HG_COLS_PER_STEP, w_q, w_f, w_i, lb, s_ctx)
    o = o_grid.reshape(2, nb, n_lat, d)

    xl, h2, eid, ew = _merge(x, y, o, h_all, w_zg, w_branch_ssm[0].astype(BF16), w_branch_hg[0].astype(BF16),
                             w_out[0].astype(BF16), g_ssm_norm, g_hg, g_post_mix, g_pre_ffn, mod,
                             w_r, b_r)

    n = nb * n_lat
    cap = -(-(2 * n + MOE_EXPERTS * MOE_ROWS) // MOE_ROWS) * MOE_ROWS
    n_blocks = cap // MOE_ROWS
    dest, blk, fill = _rank(eid.reshape(n, LANES), n_blocks)
    dest_flat = dest[:, 0:2].reshape(-1)
    xs = _dispatch(dest_flat, fill[0:2, 0:MOE_EXPERTS].reshape(-1), h2.reshape(n, d), cap)
    ys = _experts(blk[:n_blocks, 0], blk[0:1, 1], xs, w_gate[0], w_up[0], w_down[0])
    out = _combine(dest_flat, ys, xl.reshape(n, d), ew.reshape(n, LANES), mod, g_post_ffn,
                   n_lat // TOK_TILE)
    return out.reshape(nb, n_lat, d)
```

```python
import functools

import jax
import jax.numpy as jnp
from jax import lax
from jax.experimental import pallas as pl
from jax.experimental.pallas import tpu as pltpu

F32 = jnp.float32
BF16 = jnp.bfloat16
I32 = jnp.int32
HIGHEST = lax.Precision.HIGHEST

D_MODEL = 1024
GRID_W = 64
RMS_EPS = 1e-6
SSM_D_INNER = 2048
SSM_HEAD_DIM = 64
SSM_HEADS = 32
SSM_GROUPS = 4
SSM_HEADS_PER_GROUP = 8
SSM_STATE = 128
SSM_GROUP_WIDTH = SSM_HEADS_PER_GROUP * SSM_HEAD_DIM
SSM_CONV_DIM = SSM_D_INNER + 2 * SSM_GROUPS * SSM_STATE
HG_HEADS = 8
HG_DIM = 128
HG_KEY = HG_HEADS * HG_DIM
MOE_GROUPS = 4
MOE_EXPERTS_PER_GROUP = 8
MOE_EXPERTS = 32
MOE_D_FF = 512

LANES = 128
SUBLANES = 8
VMEM_LIMIT_BYTES = 56 * 1024 * 1024

TOK_TILE = 256
SSD_CHUNK = 128
HG_CHUNK = 128
HG_SUB = 16
HG_FACTOR_LOG2_RANGE = 110.0
HG_COLS_PER_STEP = 8
HG_COLS_PER_PASS = 4
MOE_ROWS = 256
RANK_TILE = 512
DMA_UNROLL = 8


def _cparams(n_axes):
    return pltpu.CompilerParams(dimension_semantics=("arbitrary",) * n_axes,
                                vmem_limit_bytes=VMEM_LIMIT_BYTES)


def _sigmoid(x):
    return 0.5 * jnp.tanh(0.5 * x) + 0.5


def _silu(x):
    return x * _sigmoid(x)


def _rms(v, g):
    ms = jnp.mean(v * v, axis=-1, keepdims=True)
    return v * lax.rsqrt(ms + RMS_EPS) * g


def _dot(a, b):
    return jnp.dot(a, b, preferred_element_type=F32)


def _dot_nt(a, b):
    return lax.dot_general(a, b, (((1,), (1,)), ((), ())), preferred_element_type=F32)


def _dot_tn(a, b):
    return lax.dot_general(a, b, (((0,), (0,)), ((), ())), preferred_element_type=F32)


def _mod_kernel(c_ref, w_ref, b_ref, o_ref):
    s = _silu(c_ref[...])
    o_ref[...] = jnp.dot(s, w_ref[...], precision=HIGHEST, preferred_element_type=F32) + b_ref[...]


def _modulation(cc, w_ada, b_ada):
    n = w_ada.shape[1]
    tn = 1024
    return pl.pallas_call(
        _mod_kernel,
        grid=(n // tn,),
        in_specs=[pl.BlockSpec((SUBLANES, D_MODEL), lambda j: (0, 0)),
                  pl.BlockSpec((D_MODEL, tn), lambda j: (0, j)),
                  pl.BlockSpec((1, tn), lambda j: (0, j))],
        out_specs=pl.BlockSpec((SUBLANES, tn), lambda j: (0, j)),
        out_shape=jax.ShapeDtypeStruct((SUBLANES, n), F32),
        compiler_params=_cparams(1),
        name="adaln_mod",
    )(cc, w_ada, b_ada.reshape(1, n))


def _norm_kernel(x_ref, ctx_ref, mod_ref, g_ref, all_ref, lat_ref, *, ctx_row):
    b = pl.program_id(0)
    i = pl.program_id(1)
    g = g_ref[...]

    def nm(v, row):
        sh = mod_ref[pl.ds(row, 1), pl.ds(0, D_MODEL)]
        sc = mod_ref[pl.ds(row, 1), pl.ds(D_MODEL, D_MODEL)]
        return _rms(v, g) * (1.0 + sc) + sh

    @pl.when(i == 0)
    def _():
        all_ref[...] = nm(ctx_ref[...], ctx_row).astype(BF16)

    @pl.when(i > 0)
    def _():
        h = nm(x_ref[...], b)
        all_ref[...] = h.astype(BF16)
        lat_ref[...] = h


def _pre_norm(x, ctx, mod, g):
    nb, n_lat, d = x.shape
    n_ctx = ctx.shape[1]
    assert n_ctx == TOK_TILE and n_lat % TOK_TILE == 0
    nt = n_lat // TOK_TILE
    return pl.pallas_call(
        functools.partial(_norm_kernel, ctx_row=nb),
        grid=(nb, nt + 1),
        in_specs=[pl.BlockSpec((None, TOK_TILE, d), lambda b, i: (b, jnp.maximum(i - 1, 0), 0)),
                  pl.BlockSpec((None, TOK_TILE, d), lambda b, i: (b, 0, 0)),
                  pl.BlockSpec(mod.shape, lambda b, i: (0, 0)),
                  pl.BlockSpec((1, d), lambda b, i: (0, 0))],
        out_specs=[pl.BlockSpec((None, TOK_TILE, d), lambda b, i: (b, i, 0)),
                   pl.BlockSpec((None, TOK_TILE, d), lambda b, i: (b, jnp.maximum(i - 1, 0), 0))],
        out_shape=[jax.ShapeDtypeStruct((nb, n_ctx + n_lat, d), BF16),
                   jax.ShapeDtypeStruct((nb, n_lat, d), F32)],
        compiler_params=_cparams(2),
        name="pre_norm",
    )(x, ctx, mod, g)


XBC_COL_TILE = 512


def _xbc_kernel(h_ref, hp_ref, hn_ref, w_ref, wdt_ref, cw_ref, cb_ref, dtb_ref, xbc_ref, dt_ref,
                raw_s, *, n_tiles):
    i = pl.program_id(1)
    h = h_ref[...]
    tm = h.shape[0]
    halo = hp_ref.shape[0]
    prev_ok = (i >= 2).astype(BF16)
    next_ok = jnp.logical_and(i >= 1, i < n_tiles - 1).astype(BF16)
    rows = jnp.concatenate([hp_ref[...] * prev_ok, h, hn_ref[...] * next_ok], axis=0)
    for c in range(SSM_CONV_DIM // XBC_COL_TILE):
        sl = slice(c * XBC_COL_TILE, (c + 1) * XBC_COL_TILE)
        raw_s[...] = _dot(rows, w_ref[:, sl])
        up = raw_s[halo - 1:halo - 1 + tm, :]
        mid = raw_s[halo:halo + tm, :]
        dn = raw_s[halo + 1:halo + 1 + tm, :]
        conv = cw_ref[0:1, sl] * up + cw_ref[1:2, sl] * mid + cw_ref[2:3, sl] * dn + cb_ref[:, sl]
        xbc_ref[:, sl] = _silu(conv).astype(BF16)
    u = _dot(h, wdt_ref[...]) + dtb_ref[...]
    dt_ref[...] = jnp.maximum(u, 0.0) + jnp.log(1.0 + jnp.exp(-jnp.abs(u)))


def _ssd_inputs(h_all, w_xbc, w_dt, conv_w, conv_b, dt_bias):
    nb, ltot, d = h_all.shape
    nt = ltot // TOK_TILE
    halo = 2 * SUBLANES
    n_halo = ltot // halo
    per = TOK_TILE // halo
    return pl.pallas_call(
        functools.partial(_xbc_kernel, n_tiles=nt),
        grid=(nb, nt),
        in_specs=[pl.BlockSpec((None, TOK_TILE, d), lambda b, i: (b, i, 0)),
                  pl.BlockSpec((None, halo, d), lambda b, i: (b, jnp.maximum(i * per - 1, 0), 0)),
                  pl.BlockSpec((None, halo, d), lambda b, i: (b, jnp.minimum((i + 1) * per, n_halo - 1), 0)),
                  pl.BlockSpec(w_xbc.shape, lambda b, i: (0, 0)),
                  pl.BlockSpec(w_dt.shape, lambda b, i: (0, 0)),
                  pl.BlockSpec(conv_w.shape, lambda b, i: (0, 0)),
                  pl.BlockSpec(conv_b.shape, lambda b, i: (0, 0)),
                  pl.BlockSpec(dt_bias.shape, lambda b, i: (0, 0))],
        out_specs=[pl.BlockSpec((None, TOK_TILE, SSM_CONV_DIM), lambda b, i: (b, i, 0)),
                   pl.BlockSpec((None, TOK_TILE, 2 * LANES), lambda b, i: (b, i, 0))],
        out_shape=[jax.ShapeDtypeStruct((nb, ltot, SSM_CONV_DIM), BF16),
                   jax.ShapeDtypeStruct((nb, ltot, 2 * LANES), F32)],
        scratch_shapes=[pltpu.VMEM((TOK_TILE + 2 * halo, XBC_COL_TILE), F32)],
        compiler_params=_cparams(2),
        name="ssd_inputs",
    )(h_all, h_all, h_all, w_xbc, w_dt, conv_w, conv_b, dt_bias)


def _ssd_kernel(xbc_ref, dt_ref, alog_ref, dskip_ref, e_ref, y_ref, state):
    d = pl.program_id(1)
    step = pl.program_id(2)
    q = SSD_CHUNK

    @pl.when(step == 0)
    def _():
        state[...] = jnp.zeros(state.shape, F32)

    lane = lax.broadcasted_iota(I32, (1, LANES), 1)
    a = jnp.where(lane < SSM_HEADS, -jnp.exp(alog_ref[...]), 0.0)
    dt = dt_ref[...]
    adt = dt * a
    r = lax.broadcasted_iota(I32, (q, q), 0)
    c = lax.broadcasted_iota(I32, (q, q), 1)
    mask = jnp.where(d == 0, r - c, c - r) >= 0
    cs = jnp.dot(mask.astype(F32), adt, precision=HIGHEST, preferred_element_type=F32)
    tot = jnp.sum(adt, axis=0, keepdims=True)
    cs_t = cs.T
    lhs = jnp.concatenate([dt, jnp.exp(cs), jnp.exp(tot - cs),
                           jnp.broadcast_to(jnp.exp(tot), (SUBLANES, LANES))], axis=0).astype(BF16)
    lane_h = lax.broadcasted_iota(I32, (q, LANES), 1)
    for g in range(SSM_GROUPS):
        gs = slice(g * SSM_GROUP_WIDTH, (g + 1) * SSM_GROUP_WIDTH)
        ex = _dot(lhs, e_ref[:, gs])
        dt_e, e1_e, e2_e, etot_e = ex[0:q], ex[q:2 * q], ex[2 * q:3 * q], ex[3 * q:3 * q + 1]
        x = xbc_ref[:, gs].astype(F32)
        bg = xbc_ref[:, SSM_D_INNER + g * SSM_STATE:SSM_D_INNER + (g + 1) * SSM_STATE]
        cg = xbc_ref[:, SSM_D_INNER + (SSM_GROUPS + g) * SSM_STATE:
                     SSM_D_INNER + (SSM_GROUPS + g + 1) * SSM_STATE]
        xdt = x * dt_e
        cb = _dot_nt(cg, bg)
        s_in = state[g]
        y_off = _dot(cg, s_in.astype(BF16)) * e1_e
        state[g] = s_in * etot_e + _dot_tn(bg, (xdt * e2_e).astype(BF16))
        parts = []
        for m in range(SSM_HEADS_PER_GROUP // 2):
            xp = xdt[:, m * LANES:(m + 1) * LANES]
            acc = None
            for half in range(2):
                hd = g * SSM_HEADS_PER_GROUP + 2 * m + half
                col = cs[:, hd:hd + 1]
                rowv = cs_t[hd:hd + 1, :]
                decay = jnp.exp(jnp.minimum(col - rowv, 0.0))
                w = jnp.where(mask, cb * decay, 0.0).astype(BF16)
                sel = (lane_h < SSM_HEAD_DIM) if half == 0 else (lane_h >= SSM_HEAD_DIM)
                t = _dot(w, jnp.where(sel, xp, 0.0).astype(BF16))
                acc = t if acc is None else acc + t
            parts.append(acc)
        y = jnp.concatenate(parts, axis=1) + y_off + x * dskip_ref[:, gs]
        y_ref[:, gs] = y.astype(BF16)


def _ssd_scan(xbc, dt, a_log_p, dskip_e, expand):
    nb, ltot, _ = xbc.shape
    q = SSD_CHUNK
    nc = ltot // q
    n_ctx_chunks = TOK_TILE // q

    def chunk(d, s):
        rev = jnp.where(s < n_ctx_chunks, n_ctx_chunks - 1 - s, nc + n_ctx_chunks - 1 - s)
        return jnp.where(d == 0, s, rev)

    return pl.pallas_call(
        _ssd_kernel,
        grid=(nb, 2, nc),
        in_specs=[pl.BlockSpec((None, q, SSM_CONV_DIM), lambda b, d, s: (b, chunk(d, s), 0)),
                  pl.BlockSpec((None, q, LANES), lambda b, d, s: (b, chunk(d, s), d)),
                  pl.BlockSpec((None, 1, LANES), lambda b, d, s: (d, 0, 0)),
                  pl.BlockSpec((None, 1, SSM_D_INNER), lambda b, d, s: (d, 0, 0)),
                  pl.BlockSpec(expand.shape, lambda b, d, s: (0, 0))],
        out_specs=pl.BlockSpec((None, None, q, SSM_D_INNER), lambda b, d, s: (d, b, chunk(d, s), 0)),
        out_shape=jax.ShapeDtypeStruct((2, nb, ltot, SSM_D_INNER), BF16),
        scratch_shapes=[pltpu.VMEM((SSM_GROUPS, SSM_STATE, SSM_GROUP_WIDTH), F32)],
        compiler_params=_cparams(3),
        name="ssd_scan",
    )(xbc, dt, a_log_p, dskip_e, expand)


def _hg_intra(q_s, k_s, b_s, r0, off):
    c = HG_CHUNK
    cols = pl.ds(off, HG_DIM)
    lane8 = lax.broadcasted_iota(I32, (SUBLANES, c), 1)
    lane_sub = lax.broadcasted_iota(I32, (HG_SUB, c), 1)
    rows = []
    for i in range(c // HG_SUB):
        lo = r0 + i * HG_SUB
        bi = b_s[lo:lo + HG_SUB, cols]
        qi = q_s[lo:lo + HG_SUB, cols]
        top = jnp.zeros((SUBLANES, c), F32)
        bot = jnp.zeros((SUBLANES, c), F32)
        for j in range(HG_SUB):
            s = lo + j
            at = i * HG_SUB + j
            brow = b_s[s:s + 1, cols]
            krow = k_s[s:s + 1, cols]
            if j < SUBLANES:
                p = qi * krow * jnp.exp2(bi - brow)
                top = jnp.where(lane8 == at, jnp.sum(p[0:SUBLANES], axis=1, keepdims=True), top)
                bot = jnp.where(lane8 == at, jnp.sum(p[SUBLANES:], axis=1, keepdims=True), bot)
            else:
                p = qi[SUBLANES:] * krow * jnp.exp2(bi[SUBLANES:] - brow)
                bot = jnp.where(lane8 == at, jnp.sum(p, axis=1, keepdims=True), bot)
        a_row = jnp.concatenate([top, bot], axis=0)
        if i > 0:
            bref = b_s[lo - 1:lo, cols]
            qt = qi * jnp.exp2(bi - bref)
            kt = k_s[r0:lo, cols] * jnp.exp2(bref - b_s[r0:lo, cols])
            ktp = jnp.concatenate([kt, jnp.zeros((c - i * HG_SUB, HG_DIM), F32)], axis=0)
            a_off = _dot_nt(qt.astype(BF16), ktp.astype(BF16))
            a_row = jnp.where(lane_sub < i * HG_SUB, a_off, a_row)
        rows.append(a_row)
    a = jnp.concatenate(rows, axis=0)
    r64 = lax.broadcasted_iota(I32, (c, c), 0)
    c64 = lax.broadcasted_iota(I32, (c, c), 1)
    return jnp.where(r64 >= c64, a, 0.0)


def _hg_kernel(h_ref, wq_ref, wf_ref, wi_ref, lb_ref, sin_ref, o_ref, sout_ref,
               pin_s, pout_s, q_s, k_s, b_s, v_s, qe_s, kd_s, ki_s, dec_s, o_s, state, *, log2_rows,
               log2_ncol, pass_rows):
    d = pl.program_id(1)
    step = pl.program_id(2)
    nsteps = pl.num_programs(2)
    t_all = 1 << (log2_rows + log2_ncol)
    c = HG_CHUNK
    nch = pass_rows // c

    def source_row(p):
        src = ((p & ((1 << log2_rows) - 1)) << log2_ncol) + (p >> log2_rows)
        return jnp.where(d == 0, src, t_all - 1 - src)

    @pl.when(step == 0)
    def _():
        state[...] = sin_ref[...]
        shape = (t_all, t_all)
        pin_s[...] = (lax.broadcasted_iota(I32, shape, 1)
                      == source_row(lax.broadcasted_iota(I32, shape, 0))).astype(BF16)
        pout_s[...] = (lax.broadcasted_iota(I32, shape, 0)
                       == source_row(lax.broadcasted_iota(I32, shape, 1))).astype(BF16)

    r64 = lax.broadcasted_iota(I32, (c, c), 0)
    c64 = lax.broadcasted_iota(I32, (c, c), 1)
    tril = (r64 >= c64).astype(BF16)
    h_rows =h_ref[...].reshape(t_all, HG_KEY).astype(BF16)

    for ps in range(t_all // pass_rows):
        prs = slice(ps * pass_rows, (ps + 1) * pass_rows)
        hb = _dot(pin_s[prs, :], h_rows).astype(BF16)
        q_s[...] = _silu(_dot(hb, wq_ref[...]))
        fz = _dot(hb, wf_ref[...])
        t = jnp.exp(-jnp.abs(fz))
        r = 1.0 / (1.0 + t)
        big, small = r, t * r
        pos = fz >= 0.0
        lb = lb_ref[...]
        lf2 = jnp.log2(lb + (1.0 - lb) * jnp.where(pos, big, small))
        k_s[...] = (1.0 - lb) * jnp.where(pos, small, big)
        v_s[...] = _dot(hb, wi_ref[...]).astype(BF16)
        for ch in range(nch):
            rs = slice(ch * c, (ch + 1) * c)
            x = lf2[rs]
            hi = x.astype(BF16)
            rem = x - hi.astype(F32)
            mid = rem.astype(BF16)
            low = (rem - mid.astype(F32)).astype(BF16)
            b = _dot(tril, hi) + _dot(tril, mid) + _dot(tril, low)
            b_s[rs, :] = b
            btot = b[c - 1:c]
            qe_s[rs, :] = (q_s[rs, :] * jnp.exp2(b)).astype(BF16)
            kd_s[rs, :] = (k_s[rs, :] * jnp.exp2(btot - b)).astype(BF16)
            ki_s[rs, :] = (k_s[rs, :] * jnp.exp2(-b)).astype(BF16)
            dec_s[ch:ch + 1, :] = jnp.exp2(btot)
            steepest = -btot if ch == 0 else jnp.maximum(steepest, -btot)
        bounded = jnp.max(steepest) <= HG_FACTOR_LOG2_RANGE

        def head(hh, carry, factored):
            off = pl.multiple_of(hh * HG_DIM, HG_DIM)
            cols = pl.ds(off, HG_DIM)
            s_t = state[hh]
            for ch in range(nch):
                rs = slice(ch * c, (ch + 1) * c)
                if factored:
                    a = jnp.where(r64 >= c64, _dot_nt(qe_s[rs, cols], ki_s[rs, cols]), 0.0)
                else:
                    a = _hg_intra(q_s, k_s, b_s, ch * c, off)
                v = v_s[rs, cols]
                o_s[rs, cols] = _dot_nt(qe_s[rs, cols], s_t.astype(BF16)) + _dot(a.astype(BF16), v)
                s_t = s_t * dec_s[ch:ch + 1, cols] + _dot_tn(v, kd_s[rs, cols])
            state[hh] = s_t
            return carry

        @pl.when(bounded)
        def _():
            lax.fori_loop(0, HG_HEADS, functools.partial(head, factored=True), 0)

        @pl.when(jnp.logical_not(bounded))
        def _():
            lax.fori_loop(0, HG_HEADS, functools.partial(head, factored=False), 0)
        back = _dot(pout_s[:, prs], o_s[...].astype(BF16)).reshape(o_ref.shape)
        if ps == 0:
            o_ref[...] = back
        else:
            o_ref[...] += back

    @pl.when(step == nsteps - 1)
    def _():
        sout_ref[...] = state[...]


def _hg_scan(h_grid, ncol, wq, wf, wi, lb, s_in):
    if h_grid.ndim == 3:
        nb, rows, d_model = h_grid.shape
        ncol_total = 1
    else:
        nb, rows, ncol_total, d_model = h_grid.shape
    nsteps = ncol_total // ncol
    t_all = rows * ncol
    pass_rows = min(t_all, HG_COLS_PER_PASS * LANES)
    log2_rows, log2_ncol = rows.bit_length() - 1, ncol.bit_length() - 1
    assert rows == 1 << log2_rows and ncol == 1 << log2_ncol and t_all % pass_rows == 0
    assert pass_rows // HG_CHUNK <= SUBLANES

    def pos(d, s):
        return jnp.where(d == 0, s, nsteps - 1 - s)

    once = pl.Buffered(1)
    if h_grid.ndim == 3:
        h_spec = pl.BlockSpec((None, rows, d_model), lambda b, d, s: (b, 0, 0))
        o_spec = pl.BlockSpec((None, None, rows, d_model), lambda b, d, s: (d, b, 0, 0))
    else:
        h_spec = pl.BlockSpec((None, rows, ncol, d_model), lambda b, d, s: (b, 0, pos(d, s), 0))
        o_spec = pl.BlockSpec((None, None, rows, ncol, d_model), lambda b, d, s: (d, b, 0, pos(d, s), 0))
    st_spec = pl.BlockSpec((None, None, HG_HEADS, HG_DIM, HG_DIM), lambda b, d, s: (b, d, 0, 0, 0))
    wide_f32 = pltpu.VMEM((pass_rows, HG_KEY), F32)
    wide_bf16 = pltpu.VMEM((pass_rows, HG_KEY), BF16)
    perm = pltpu.VMEM((t_all, t_all), BF16)
    return pl.pallas_call(
        functools.partial(_hg_kernel, log2_rows=log2_rows, log2_ncol=log2_ncol, pass_rows=pass_rows),
        grid=(nb, 2, nsteps),
        in_specs=[h_spec,
                  pl.BlockSpec(wq.shape, lambda b, d, s: (0, 0), pipeline_mode=once),
                  pl.BlockSpec((None, d_model, HG_KEY), lambda b, d, s: (d, 0, 0), pipeline_mode=once),
                  pl.BlockSpec(wi.shape, lambda b, d, s: (0, 0), pipeline_mode=once),
                  pl.BlockSpec((None, 1, HG_KEY), lambda b, d, s: (d, 0, 0)),
                  st_spec],
        out_specs=[o_spec, st_spec],
        out_shape=[jax.ShapeDtypeStruct((2,) + h_grid.shape, F32),
                   jax.ShapeDtypeStruct(s_in.shape, F32)],
        scratch_shapes=[perm, perm, wide_f32, wide_f32, wide_f32, wide_bf16, wide_bf16, wide_bf16, wide_bf16,
                        pltpu.VMEM((SUBLANES, HG_KEY), F32), wide_f32,
                        pltpu.VMEM((HG_HEADS, HG_DIM, HG_DIM), F32)],
        compiler_params=_cparams(3),
        name="hgrn2_scan",
    )(h_grid, wq, wf, wi, lb, s_in)


def _merge_kernel(x_ref, yf_ref, yb_ref, of_ref, ob_ref, h_ref, wzg_ref, wbs_ref, wbh_ref, wo_ref,
                  gs_ref, gh_ref, gpm_ref, gpf_ref, mod_ref, wr_ref, br_ref,
                  xl_ref, h2_ref, eid_ref, ew_ref):
    b = pl.program_id(0)
    tm = x_ref.shape[0]
    d = D_MODEL
    h = h_ref[...]
    z = _dot(h, wzg_ref[:, 0:SSM_D_INNER])
    yz = (yf_ref[...].astype(F32) + yb_ref[...].astype(F32)) * _silu(z)
    parts = []
    for g in range(SSM_GROUPS):
        gsl = slice(g * SSM_GROUP_WIDTH, (g + 1) * SSM_GROUP_WIDTH)
        parts.append(_rms(yz[:, gsl], gs_ref[:, gsl]).astype(BF16))
    br_ssm = _dot(jnp.concatenate(parts, axis=1), wbs_ref[...])

    o = of_ref[...].astype(F32) + ob_ref[...].astype(F32)
    og = _dot(h, wzg_ref[:, SSM_D_INNER:SSM_D_INNER + d])
    parts = []
    for hh in range(HG_HEADS):
        hsl = slice(hh * HG_DIM, (hh + 1) * HG_DIM)
        parts.append((_rms(o[:, hsl], gh_ref[...]) * _silu(og[:, hsl])).astype(BF16))
    br_hg = _dot(jnp.concatenate(parts, axis=1), wbh_ref[...])

    gm = _dot(h, wzg_ref[:, SSM_D_INNER + d:SSM_D_INNER + 2 * d])
    gh = _dot(h, wzg_ref[:, SSM_D_INNER + 2 * d:SSM_D_INNER + 3 * d])
    mixed = _sigmoid(gm) * br_ssm + _sigmoid(gh) * br_hg
    mix = _dot(mixed.astype(BF16), wo_ref[...])

    gt1 = mod_ref[pl.ds(b, 1), pl.ds(2 * d, d)]
    sh2 = mod_ref[pl.ds(b, 1), pl.ds(3 * d, d)]
    sc2 = mod_ref[pl.ds(b, 1), pl.ds(4 * d, d)]
    xl = x_ref[...] + gt1 * _rms(mix, gpm_ref[...])
    xl_ref[...] = xl
    h2 = _rms(xl, gpf_ref[...]) * (1.0 + sc2) + sh2
    h2_ref[...] = h2

    logits = jnp.dot(h2, wr_ref[...], precision=HIGHEST, preferred_element_type=F32) + br_ref[...]
    lane = lax.broadcasted_iota(I32, (tm, LANES), 1)
    lane_f = lane.astype(F32)
    neg = jnp.float32(-jnp.inf)
    far = jnp.float32(2 * LANES)
    gl = jnp.where(lane < MOE_GROUPS, logits, neg)
    gmax = jnp.max(gl, axis=1, keepdims=True)
    p_grp = 1.0 / jnp.sum(jnp.exp(gl - gmax), axis=1, keepdims=True)
    grp = jnp.min(jnp.where(gl == gmax, lane_f, far), axis=1, keepdims=True)
    lo = MOE_GROUPS + MOE_EXPERTS_PER_GROUP * grp
    el = jnp.where(jnp.logical_and(lane_f >= lo, lane_f < lo + MOE_EXPERTS_PER_GROUP), logits, neg)
    m1 = jnp.max(el, axis=1, keepdims=True)
    i1 = jnp.min(jnp.where(el == m1, lane_f, far), axis=1, keepdims=True)
    el2 = jnp.where(lane_f == i1, neg, el)
    m2 = jnp.max(el2, axis=1, keepdims=True)
    i2 = jnp.min(jnp.where(el2 == m2, lane_f, far), axis=1, keepdims=True)
    e2 = jnp.exp(m2 - m1)
    w1 = p_grp / (1.0 + e2)
    w2 = p_grp * e2 / (1.0 + e2)
    eid = jnp.where(lane == 0, i1, i2) - float(MOE_GROUPS)
    eid_ref[...] = jnp.where(lane < 2, eid, 0.0).astype(I32)
    ew_ref[...] = jnp.where(lane == 0, w1, jnp.where(lane == 1, w2, 0.0))


def _merge(x, y, o, h_all, w_zg, w_bs, w_bh, w_o, g_ssm, g_hg, g_pm, g_pf, mod, w_r, b_r):
    nb, n_lat, d = x.shape
    tm = TOK_TILE
    ctx_tiles = (y.shape[2] - n_lat) // tm

    def full(a):
        return pl.BlockSpec(a.shape, lambda b, i: (0,) * a.ndim, pipeline_mode=pl.Buffered(1))

    tok = lambda w: pl.BlockSpec((None, tm, w), lambda b, i: (b, i, 0))
    return pl.pallas_call(
        _merge_kernel,
        grid=(nb, n_lat // tm),
        in_specs=[tok(d),
                  pl.BlockSpec((None, None, tm, SSM_D_INNER), lambda b, i: (0, b, i + ctx_tiles, 0)),
                  pl.BlockSpec((None, None, tm, SSM_D_INNER), lambda b, i: (1, b, i + ctx_tiles, 0)),
                  pl.BlockSpec((None, None, tm, d), lambda b, i: (0, b, i, 0)),
                  pl.BlockSpec((None, None, tm, d), lambda b, i: (1, b, i, 0)),
                  pl.BlockSpec((None, tm, d), lambda b, i: (b, i + ctx_tiles, 0)),
                  full(w_zg), full(w_bs), full(w_bh), full(w_o), full(g_ssm), full(g_hg), full(g_pm),
                  full(g_pf), full(mod), full(w_r), full(b_r)],
        out_specs=[tok(d), tok(d), tok(LANES), tok(LANES)],
        out_shape=[jax.ShapeDtypeStruct((nb, n_lat, d), F32),
                   jax.ShapeDtypeStruct((nb, n_lat, d), F32),
                   jax.ShapeDtypeStruct((nb, n_lat, LANES), I32),
                   jax.ShapeDtypeStruct((nb, n_lat, LANES), F32)],
        compiler_params=_cparams(2),
        name="merge_route",
    )(x, y, y, o, o, h_all, w_zg, w_bs, w_bh, w_o, g_ssm, g_hg, g_pm, g_pf, mod, w_r, b_r)


def _rank_kernel(eid_ref, dest_ref, blk_ref, fill_ref, cnt, base):
    p = pl.program_id(0)
    i = pl.program_id(1)
    t = eid_ref.shape[0]
    lane = lax.broadcasted_iota(I32, (t, LANES), 1)
    e = eid_ref[...]
    oh0 = (lane == e[:, 0:1]).astype(F32)
    oh1 = (lane == e[:, 1:2]).astype(F32)
    c0 = jnp.sum(oh0, axis=0, keepdims=True)
    c1 = jnp.sum(oh1, axis=0, keepdims=True)

    @pl.when(jnp.logical_and(p == 0, i == 0))
    def _():
        cnt[...] = jnp.zeros(cnt.shape, F32)

    @pl.when(p == 0)
    def _():
        cnt[...] = cnt[...] + c0 + c1

    @pl.when(jnp.logical_and(p == 1, i == 0))
    def _():
        lane1 = lax.broadcasted_iota(I32, (SUBLANES, LANES), 1)
        padded = jnp.floor((cnt[...] + (MOE_ROWS - 1)) / MOE_ROWS) * MOE_ROWS
        padded = jnp.broadcast_to(jnp.where(lane1[0:1] < MOE_EXPERTS, padded, 0.0), (SUBLANES, LANES))
        rr = lax.broadcasted_iota(I32, (LANES, LANES), 0)
        cc = lax.broadcasted_iota(I32, (LANES, LANES), 1)
        start = jnp.dot(padded, (rr < cc).astype(F32), precision=HIGHEST, preferred_element_type=F32)
        base[...] = start[0:1]
        pad_end = start[0:1] + padded[0:1]
        pad_end = jnp.where(lane1[0:1] < MOE_EXPERTS, pad_end, jnp.float32(2.0 ** 30))
        nblk = blk_ref.shape[0]
        bstart = lax.broadcasted_iota(I32, (nblk, LANES), 0).astype(F32) * MOE_ROWS
        owner = jnp.sum((pad_end <= bstart).astype(F32), axis=1, keepdims=True)
        owner = jnp.minimum(owner, MOE_EXPERTS - 1)
        total = jnp.max(jnp.where(lane1[0:1] == MOE_EXPERTS - 1, pad_end, 0.0), axis=1, keepdims=True)
        lane_b = lax.broadcasted_iota(I32, (nblk, LANES), 1)
        blk_ref[...] = jnp.where(lane_b == 0, owner, total / MOE_ROWS).astype(I32)
        sub = lax.broadcasted_iota(I32, (SUBLANES, LANES), 0)
        fill_ref[...] = jnp.where(sub == 0, start + cnt[...], start + padded).astype(I32)

    @pl.when(p == 1)
    def _():
        rr = lax.broadcasted_iota(I32, (t, t), 0)
        cc = lax.broadcasted_iota(I32, (t, t), 1)
        before = (rr > cc).astype(BF16)
        b0 = base[...]
        r0 = _dot(before, oh0.astype(BF16)) + b0
        r1 = _dot(before, oh1.astype(BF16)) + b0 + c0
        d0 = jnp.sum(oh0 * r0, axis=1, keepdims=True)
        d1 = jnp.sum(oh1 * r1, axis=1, keepdims=True)
        dest_ref[...] = jnp.where(lane == 0, d0, jnp.where(lane == 1, d1, 0.0)).astype(I32)
        base[...] = b0 + c0 + c1


def _rank(eid, n_blocks):
    n = eid.shape[0]
    t = RANK_TILE
    nblk_pad = -(-n_blocks // SUBLANES) * SUBLANES
    return pl.pallas_call(
        _rank_kernel,
        grid=(2, n // t),
        in_specs=[pl.BlockSpec((t, LANES), lambda p, i: (i, 0))],
        out_specs=[pl.BlockSpec((t, LANES), lambda p, i: (p * i, 0)),
                   pl.BlockSpec((nblk_pad, LANES), lambda p, i: (0, 0)),
                   pl.BlockSpec((SUBLANES, LANES), lambda p, i: (0, 0))],
        out_shape=[jax.ShapeDtypeStruct((n, LANES), I32),
                   jax.ShapeDtypeStruct((nblk_pad, LANES), I32),
                   jax.ShapeDtypeStruct((SUBLANES, LANES), I32)],
        scratch_shapes=[pltpu.VMEM((1, LANES), F32), pltpu.VMEM((1, LANES), F32)],
        compiler_params=_cparams(2),
        name="moe_rank",
    )(eid)


def _row_copy(src, s, dst, t, sem):
    return pltpu.make_async_copy(src.at[pl.ds(s, 1)], dst.at[pl.ds(t, 1)], sem)


def _dispatch_kernel(dest_ref, fill_ref, h_ref, out_ref, zero_row, sem):
    i = pl.program_id(0)
    t = h_ref.shape[0]
    blk_rows = zero_row.shape[0]

    @pl.when(i == 0)
    def _():
        zero_row[...] = jnp.zeros(zero_row.shape, F32)
        first_free = fill_ref[2 * MOE_EXPERTS - 1] // blk_rows
        n_blocks = out_ref.shape[0] // blk_rows

        def blk_copy(b):
            return pltpu.make_async_copy(zero_row, out_ref.at[pl.ds(b * blk_rows, blk_rows)], sem)

        def fill_blk(b, c):
            blk_copy(b).start()
            return c

        def done_blk(b, c):
            blk_copy(0).wait()
            return c

        lax.fori_loop(first_free, n_blocks, fill_blk, 0)
        lax.fori_loop(first_free, n_blocks, done_blk, 0)

        def fill(r, c):
            _row_copy(zero_row, 0, out_ref, r, sem).start()
            return c

        def done(r, c):
            _row_copy(zero_row, 0, out_ref, 0, sem).wait()
            return c

        def per_expert(e, n_started):
            lo = fill_ref[e]
            hi = fill_ref[MOE_EXPERTS + e]
            lax.fori_loop(lo, hi, fill, 0)
            return n_started + (hi - lo)

        lax.fori_loop(0, lax.fori_loop(0, MOE_EXPERTS, per_expert, 0), done, 0)

    def start(r, carry):
        base = (i * t + r) * 2
        _row_copy(h_ref, r, out_ref, dest_ref[base], sem).start()
        _row_copy(h_ref, r, out_ref, dest_ref[base + 1], sem).start()
        return carry

    lax.fori_loop(0, t, start, 0, unroll=DMA_UNROLL)

    def wait(r, carry):
        _row_copy(h_ref, 0, out_ref, 0, sem).wait()
        _row_copy(h_ref, 0, out_ref, 0, sem).wait()
        return carry

    lax.fori_loop(0, t, wait, 0, unroll=DMA_UNROLL)


def _dispatch(dest_flat, fill_flat, h2, cap):
    n, d = h2.shape
    t = TOK_TILE
    return pl.pallas_call(
        _dispatch_kernel,
        grid_spec=pltpu.PrefetchScalarGridSpec(
            num_scalar_prefetch=2,
            grid=(n // t,),
            in_specs=[pl.BlockSpec((t, d), lambda i, dest, fill: (i, 0))],
            out_specs=pl.BlockSpec(memory_space=pl.ANY),
            scratch_shapes=[pltpu.VMEM((MOE_ROWS, d), F32), pltpu.SemaphoreType.DMA]),
        out_shape=jax.ShapeDtypeStruct((cap, d), F32),
        compiler_params=_cparams(1),
        name="moe_dispatch",
    )(dest_flat, fill_flat, h2)


def _expert_kernel(be_ref, na_ref, x_ref, wg_ref, wu_ref, wd_ref, y_ref, wg_s, wu_s, wd_s):
    i = pl.program_id(0)
    e = be_ref[i]
    prev = be_ref[jnp.maximum(i - 1, 0)]

    @pl.when(jnp.logical_or(i == 0, e != prev))
    def _():
        wg_s[...] = wg_ref[...].astype(BF16)
        wu_s[...] = wu_ref[...].astype(BF16)
        wd_s[...] = wd_ref[...].astype(BF16)

    @pl.when(i < na_ref[0])
    def _():
        x = x_ref[...].astype(BF16)
        hid = _silu(_dot(x, wg_s[...])) * _dot(x, wu_s[...])
        y_ref[...] = _dot(hid.astype(BF16), wd_s[...])

    @pl.when(i >= na_ref[0])
    def _():
        y_ref[...] = jnp.zeros(y_ref.shape, F32)


def _experts(blk_e, n_act, xs, w_gate, w_up, w_down):
    cap, d = xs.shape
    r = MOE_ROWS
    ff = w_gate.shape[2]
    return pl.pallas_call(
        _expert_kernel,
        grid_spec=pltpu.PrefetchScalarGridSpec(
            num_scalar_prefetch=2,
            grid=(cap // r,),
            in_specs=[pl.BlockSpec((r, d), lambda i, be, na: (jnp.where(i < na[0], i, 0), 0)),
                      pl.BlockSpec((None, d, ff), lambda i, be, na: (be[i], 0, 0)),
                      pl.BlockSpec((None, d, ff), lambda i, be, na: (be[i], 0, 0)),
                      pl.BlockSpec((None, ff, d), lambda i, be, na: (be[i], 0, 0))],
            out_specs=pl.BlockSpec((r, d), lambda i, be, na: (i, 0)),
            scratch_shapes=[pltpu.VMEM((d, ff), BF16), pltpu.VMEM((d, ff), BF16),
                            pltpu.VMEM((ff, d), BF16)]),
        out_shape=jax.ShapeDtypeStruct((cap, d), F32),
        compiler_params=_cparams(1),
        name="moe_experts",
    )(blk_e, n_act, xs, w_gate, w_up, w_down)


def _combine_kernel(dest_ref, ys_ref, xl_ref, ew_ref, mod_ref, g_ref, out_ref, buf, sems, *, tiles_per_batch):
    i = pl.program_id(0)
    n_steps = pl.num_programs(0)
    t = xl_ref.shape[0]

    def gather(tile, slot):
        def start(r, carry):
            base = (tile * t + r) * 2
            _row_copy(ys_ref, dest_ref[base], buf.at[slot, 0], r, sems.at[slot]).start()
            _row_copy(ys_ref, dest_ref[base + 1], buf.at[slot, 1], r, sems.at[slot]).start()
            return carry

        lax.fori_loop(0, t, start, 0, unroll=DMA_UNROLL)

    @pl.when(i == 0)
    def _():
        gather(0, 0)

    @pl.when(i + 1 < n_steps)
    def _():
        gather(i + 1, (i + 1) % 2)

    slot = i % 2

    def wait(r, carry):
        _row_copy(ys_ref, 0, buf.at[slot, 0], 0, sems.at[slot]).wait()
        _row_copy(ys_ref, 0, buf.at[slot, 1], 0, sems.at[slot]).wait()
        return carry

    lax.fori_loop(0, t, wait, 0, unroll=DMA_UNROLL)
    b = i // tiles_per_batch
    ew = ew_ref[...]
    ffn = buf[slot, 0] * ew[:, 0:1] + buf[slot, 1] * ew[:, 1:2]
    gt2 = mod_ref[pl.ds(b, 1), pl.ds(5 * D_MODEL, D_MODEL)]
    out_ref[...] = xl_ref[...] + gt2 * _rms(ffn, g_ref[...])


def _combine(dest_flat, ys, xl, ew, mod, g, tiles_per_batch):
    n, d = xl.shape
    t = TOK_TILE
    return pl.pallas_call(
        functools.partial(_combine_kernel, tiles_per_batch=tiles_per_batch),
        grid_spec=pltpu.PrefetchScalarGridSpec(
            num_scalar_prefetch=1,
            grid=(n // t,),
            in_specs=[pl.BlockSpec(memory_space=pl.ANY),
                      pl.BlockSpec((t, d), lambda i, dest: (i, 0)),
                      pl.BlockSpec((t, LANES), lambda i, dest: (i, 0)),
                      pl.BlockSpec(mod.shape, lambda i, dest: (0, 0)),
                      pl.BlockSpec(g.shape, lambda i, dest: (0, 0))],
            out_specs=pl.BlockSpec((t, d), lambda i, dest: (i, 0)),
            scratch_shapes=[pltpu.VMEM((2, 2, t, d), F32), pltpu.SemaphoreType.DMA((2,))]),
        out_shape=jax.ShapeDtypeStruct((n, d), F32),
        compiler_params=_cparams(1),
        name="moe_combine",
    )(dest_flat, ys, xl, ew, mod, g)


def kernel(x, c, ctx, c_ctx, w_ada, b_ada, g_pre_mix, g_post_mix, g_pre_ffn, g_post_ffn, w_in,
           conv_w, conv_b, dt_bias, a_log, d_skip, g_ssm_norm, hg_lb, g_hg_norm, w_branch_ssm,
           w_branch_hg, w_out, w_group_router, b_group_router, w_expert_router, b_expert_router,
           w_gate, w_up, w_down):
    nb, n_lat, d = x.shape
    n_ctx = ctx.shape[1]
    rows = n_lat // GRID_W
    assert w_ada.shape[0] == 1 and d == D_MODEL and nb + 1 <= SUBLANES
    assert rows % HG_CHUNK == 0 and n_ctx % (2 * HG_CHUNK) == 0

    w = w_in[0]
    o0 = 0
    w_z = w[:, o0:o0 + SSM_D_INNER]; o0 += SSM_D_INNER
    w_xbc = w[:, o0:o0 + SSM_CONV_DIM].astype(BF16); o0 += SSM_CONV_DIM
    w_dtr = w[:, o0:o0 + 2 * SSM_HEADS]; o0 += 2 * SSM_HEADS
    w_q = w[:, o0:o0 + HG_KEY].astype(BF16); o0 += HG_KEY
    w_f = jnp.stack([w[:, o0:o0 + HG_KEY], w[:, o0 + HG_KEY:o0 + 2 * HG_KEY]]).astype(BF16); o0 += 2 * HG_KEY
    w_i = w[:, o0:o0 + HG_KEY].astype(BF16); o0 += HG_KEY
    w_zg = jnp.concatenate([w_z, w[:, o0:o0 + 3 * d]], axis=1).astype(BF16)
    pad = jnp.zeros((d, LANES - SSM_HEADS), F32)
    w_dt = jnp.concatenate([w_dtr[:, :SSM_HEADS], pad, w_dtr[:, SSM_HEADS:], pad], axis=1).astype(BF16)
    zpad = jnp.zeros((LANES - SSM_HEADS,), F32)
    dtb = jnp.concatenate([dt_bias[0, 0], zpad, dt_bias[0, 1], zpad]).reshape(1, 2 * LANES)
    a_log_p = jnp.pad(a_log[0], ((0, 0), (0, LANES - SSM_HEADS))).reshape(2, 1, LANES)
    dskip_e = jnp.repeat(d_skip[0], SSM_HEAD_DIM, axis=-1).reshape(2, 1, SSM_D_INNER)
    head_of_lane = jnp.arange(SSM_D_INNER, dtype=I32) // SSM_HEAD_DIM
    expand = (jnp.arange(LANES, dtype=I32)[:, None] == head_of_lane[None, :]).astype(BF16)
    lb = jax.nn.softmax(hg_lb.astype(F32), axis=0)[0].reshape(2, 1, HG_KEY)
    w_r = jnp.concatenate([w_group_router[0],
                           jnp.transpose(w_expert_router[0], (1, 0, 2)).reshape(d, MOE_EXPERTS),
                           jnp.zeros((d, LANES - MOE_GROUPS - MOE_EXPERTS), F32)], axis=1)
    b_r = jnp.concatenate([b_group_router[0], b_expert_router[0].reshape(-1),
                           jnp.zeros((LANES - MOE_GROUPS - MOE_EXPERTS,), F32)]).reshape(1, LANES)
    g_hg = g_hg_norm[0].reshape(1, HG_DIM)

    cc = jnp.concatenate([c, c_ctx[None], jnp.zeros((SUBLANES - nb - 1, d), F32)], axis=0)
    mod = _modulation(cc, w_ada[0], b_ada[0])
    h_all, h_lat = _pre_norm(x, ctx, mod, g_pre_mix)

    xbc, dt = _ssd_inputs(h_all, w_xbc, w_dt, conv_w[0], conv_b, dtb)
    y = _ssd_scan(xbc, dt, a_log_p, dskip_e, expand)

    s0 = jnp.zeros((nb, 2, HG_HEADS, HG_DIM, HG_DIM), F32)
    _, s_ctx = _hg_scan(h_all[:, :n_ctx].astype(F32), 1, w_q, w_f, w_i, lb, s0)
    o_grid, _ = _hg_scan(h_lat.reshape(nb, rows, GRID_W, d), HG_COLS_PER_STEP, w_q, w_f, w_i, lb, s_ctx)
    o = o_grid.reshape(2, nb, n_lat, d)

    xl, h2, eid, ew = _merge(x, y, o, h_all, w_zg, w_branch_ssm[0].astype(BF16), w_branch_hg[0].astype(BF16),
                             w_out[0].astype(BF16), g_ssm_norm, g_hg, g_post_mix, g_pre_ffn, mod,
                             w_r, b_r)

    n = nb * n_lat
    cap = -(-(2 * n + MOE_EXPERTS * MOE_ROWS) // MOE_ROWS) * MOE_ROWS
    n_blocks = cap // MOE_ROWS
    dest, blk, fill = _rank(eid.reshape(n, LANES), n_blocks)
    dest_flat = dest[:, 0:2].reshape(-1)
    xs = _dispatch(dest_flat, fill[0:2, 0:MOE_EXPERTS].reshape(-1), h2.reshape(n, d), cap)
    ys = _experts(blk[:n_blocks, 0], blk[0:1, 1], xs, w_gate[0], w_up[0], w_down[0])
    out = _combine(dest_flat, ys, xl.reshape(n, d), ew.reshape(n, LANES), mod, g_post_ffn,
                   n_lat // TOK_TILE)
    return out.reshape(nb, n_lat, d)
```
